```python
import math
import jax, jax.numpy as jnp
from jax import lax
import numpy as np

D_MODEL = 1024
BATCH = 8
SEQ = 2048
DEPTH = 2
DEC_BATCH = 128
DEC_SEQ = 1
PAST_LEN = 16384
PAGE_SIZE = 128

MIX_WIDTH = D_MODEL
S5_WIDTH = MIX_WIDTH // 2
S5_GROUP = 16
S5_GROUPS = S5_WIDTH // S5_GROUP
S5_STATE = 64
CONV_CH = MIX_WIDTH - S5_WIDTH
CONV_HEADS = 8
CONV_K = 31
MIX_IN = S5_WIDTH + 2 * CONV_CH
D_FF = 2816
FFN_K = 3
EPS = 1e-6
DT_MIN = 1e-3
DT_MAX = 1e-1

kernel_name = "hybrid_s5_conformer_convffn_step"


def rmsnorm(x, g):
    xf = x.astype(jnp.float32)
    y = xf * lax.rsqrt(jnp.mean(xf * xf, axis=-1, keepdims=True) + EPS)
    return (y * g.astype(jnp.float32)).astype(x.dtype)


def layernorm(x, g, b):
    xf = x.astype(jnp.float32)
    mu = jnp.mean(xf, axis=-1, keepdims=True)
    xc = xf - mu
    y = xc * lax.rsqrt(jnp.mean(xc * xc, axis=-1, keepdims=True) + EPS)
    return (y * g.astype(jnp.float32) + b.astype(jnp.float32)).astype(x.dtype)


def causal_dwconv(x, buf, w, b):
    k = w.shape[0]
    xp = jnp.concatenate([buf.astype(x.dtype), x], axis=1)
    y = lax.conv_general_dilated(
        xp, w[:, None, :].astype(x.dtype), window_strides=(1,), padding='VALID',
        dimension_numbers=('NWC', 'WIO', 'NWC'), feature_group_count=x.shape[-1])
    return y + b.astype(x.dtype), xp[:, xp.shape[1] - (k - 1):]


def s5_discretise(lam_re, lam_im, log_dt, b_re, b_im):
    f32 = jnp.float32
    dt = jnp.exp(log_dt.astype(f32))[:, None]
    lr, li = lam_re.astype(f32), lam_im.astype(f32)
    mag = jnp.exp(lr * dt)
    ar, ai = mag * jnp.cos(li * dt), mag * jnp.sin(li * dt)
    den = lr * lr + li * li
    cr = ((ar - 1.0) * lr + ai * li) / den
    ci = (ai * lr - (ar - 1.0) * li) / den
    br, bi = b_re.astype(f32), b_im.astype(f32)
    bbar_r = cr[..., None] * br - ci[..., None] * bi
    bbar_i = cr[..., None] * bi + ci[..., None] * br
    return ar, ai, bbar_r, bbar_i


def _cplx_combine(e1, e2):
    a1r, a1i, b1r, b1i = e1
    a2r, a2i, b2r, b2i = e2
    return (a2r * a1r - a2i * a1i, a2r * a1i + a2i * a1r,
            a2r * b1r - a2i * b1i + b2r, a2r * b1i + a2i * b1r + b2i)


def s5_mixer(u, h0_re, h0_im, lam_re, lam_im, log_dt, b_re, b_im, c_re, c_im, d_skip, w_glu, b_glu):
    bsz, L = u.shape[0], u.shape[1]
    uf = u.astype(jnp.float32)
    ug = uf.reshape(bsz, L, S5_GROUPS, S5_GROUP)
    ar, ai, bbr, bbi = s5_discretise(lam_re, lam_im, log_dt, b_re, b_im)
    xr = jnp.einsum('blgh,gph->blgp', ug, bbr)
    xi = jnp.einsum('blgh,gph->blgp', ug, bbi)
    h0r, h0i = h0_re.astype(jnp.float32), h0_im.astype(jnp.float32)
    xr = xr.at[:, 0].add(ar * h0r - ai * h0i)
    xi = xi.at[:, 0].add(ar * h0i + ai * h0r)
    a_r = jnp.broadcast_to(ar, xr.shape)
    a_i = jnp.broadcast_to(ai, xi.shape)
    _, _, hr, hi = lax.associative_scan(_cplx_combine, (a_r, a_i, xr, xi), axis=1)
    y = (jnp.einsum('blgp,ghp->blgh', hr, c_re.astype(jnp.float32))
         - jnp.einsum('blgp,ghp->blgh', hi, c_im.astype(jnp.float32)))
    y = y.reshape(bsz, L, S5_WIDTH) + d_skip.astype(jnp.float32) * uf
    a = jax.nn.gelu(y, approximate=False)
    out = a * jax.nn.sigmoid(a @ w_glu.astype(jnp.float32) + b_glu.astype(jnp.float32))
    return out.astype(u.dtype), hr[:, -1], hi[:, -1]


def conformer_conv(v, g, buf, conv_w, conv_b, ln_g, ln_b):
    z = v * jax.nn.sigmoid(g)
    c, new_buf = causal_dwconv(z, buf, conv_w, conv_b)
    return jax.nn.silu(layernorm(c, ln_g, ln_b)), new_buf


def run_trunk(x, ssm_re, ssm_im, conv_buf, ffn_buf, W):
    new_re, new_im, new_conv, new_ffn = [], [], [], []
    for l in range(DEPTH):
        h = rmsnorm(x, W['g_pre_mix'][l])
        p = h @ W['w_in'][l]
        u = p[..., :S5_WIDTH]
        cv = p[..., S5_WIDTH:S5_WIDTH + CONV_CH]
        cg = p[..., S5_WIDTH + CONV_CH:]
        s, hr, hi = s5_mixer(u, ssm_re[l], ssm_im[l], W['lam_re'][l], W['lam_im'][l], W['log_dt'][l],
                             W['b_re'][l], W['b_im'][l], W['c_re'][l], W['c_im'][l],
                             W['d_skip'][l], W['w_glu'][l], W['b_glu'][l])
        c, cbuf = conformer_conv(cv, cg, conv_buf[l], W['conv_w'][l], W['conv_b'][l],
                                 W['ln_g'][l], W['ln_b'][l])
        mix = jnp.concatenate([s, c], axis=-1) @ W['w_out'][l]
        x = x + rmsnorm(mix, W['g_post_mix'][l])
        h = rmsnorm(x, W['g_pre_ffn'][l])
        up = h @ W['w_up'][l]
        up, fbuf = causal_dwconv(up, ffn_buf[l], W['ffn_conv_w'][l], W['ffn_conv_b'][l])
        f = (jax.nn.silu(up[..., :D_FF]) * up[..., D_FF:]) @ W['w_down'][l]
        x = x + rmsnorm(f, W['g_post_ffn'][l])
        new_re.append(hr); new_im.append(hi); new_conv.append(cbuf); new_ffn.append(fbuf)
    return x, jnp.stack(new_re), jnp.stack(new_im), jnp.stack(new_conv), jnp.stack(new_ffn)


def setup_inputs(seed: int = 0) -> dict:
    key = jax.random.key(seed)
    ks = iter(jax.random.split(key, 40))
    nrm = lambda shape, s: jax.random.normal(next(ks), shape, jnp.float32) * s
    gain = lambda shape: 1.0 + nrm(shape, 0.02)
    n_idx = jnp.arange(S5_STATE, dtype=jnp.float32)
    inp = {}
    inp['x_prompt'] = nrm((BATCH, SEQ, D_MODEL), 1.0)
    inp['x_sample'] = nrm((DEC_BATCH, DEC_SEQ, D_MODEL), 1.0)
    inp['state_ssm_re'] = nrm((DEPTH, DEC_BATCH, S5_GROUPS, S5_STATE), 0.1)
    inp['state_ssm_im'] = nrm((DEPTH, DEC_BATCH, S5_GROUPS, S5_STATE), 0.1)
    inp['state_conv'] = nrm((DEPTH, DEC_BATCH, CONV_K - 1, CONV_CH), 0.5)
    inp['state_ffn_conv'] = nrm((DEPTH, DEC_BATCH, FFN_K - 1, 2 * D_FF), 1.0)
    inp['g_pre_mix'] = gain((DEPTH, D_MODEL))
    inp['w_in'] = nrm((DEPTH, D_MODEL, MIX_IN), D_MODEL ** -0.5)
    inp['lam_re'] = -0.5 + nrm((DEPTH, S5_GROUPS, S5_STATE), 0.01)
    inp['lam_im'] = math.pi * n_idx + nrm((DEPTH, S5_GROUPS, S5_STATE), 0.01)
    inp['log_dt'] = jax.random.uniform(next(ks), (DEPTH, S5_GROUPS), jnp.float32,
                                       math.log(DT_MIN), math.log(DT_MAX))
    bs = (2.0 * S5_GROUP) ** -0.5
    inp['b_re'] = nrm((DEPTH, S5_GROUPS, S5_STATE, S5_GROUP), bs)
    inp['b_im'] = nrm((DEPTH, S5_GROUPS, S5_STATE, S5_GROUP), bs)
    cs = (2.0 * S5_STATE) ** -0.5
    inp['c_re'] = nrm((DEPTH, S5_GROUPS, S5_GROUP, S5_STATE), cs)
    inp['c_im'] = nrm((DEPTH, S5_GROUPS, S5_GROUP, S5_STATE), cs)
    inp['d_skip'] = nrm((DEPTH, S5_WIDTH), 1.0)
    inp['w_glu'] = nrm((DEPTH, S5_WIDTH, S5_WIDTH), S5_WIDTH ** -0.5)
    inp['b_glu'] = nrm((DEPTH, S5_WIDTH), 0.02)
    inp['conv_w'] = nrm((DEPTH, CONV_K, CONV_CH), CONV_K ** -0.5)
    inp['conv_b'] = nrm((DEPTH, CONV_CH), 0.02)
    inp['ln_g'] = gain((DEPTH, CONV_CH))
    inp['ln_b'] = nrm((DEPTH, CONV_CH), 0.02)
    inp['w_out'] = nrm((DEPTH, MIX_WIDTH, D_MODEL), MIX_WIDTH ** -0.5)
    inp['g_post_mix'] = gain((DEPTH, D_MODEL))
    inp['g_pre_ffn'] = gain((DEPTH, D_MODEL))
    inp['w_up'] = nrm((DEPTH, D_MODEL, 2 * D_FF), D_MODEL ** -0.5)
    inp['ffn_conv_w'] = nrm((DEPTH, FFN_K, 2 * D_FF), FFN_K ** -0.5)
    inp['ffn_conv_b'] = nrm((DEPTH, 2 * D_FF), 0.02)
    inp['w_down'] = nrm((DEPTH, D_FF, D_MODEL), D_FF ** -0.5)
    inp['g_post_ffn'] = gain((DEPTH, D_MODEL))
    return inp


def reference(x_prompt, x_sample, state_ssm_re, state_ssm_im, state_conv, state_ffn_conv,
              g_pre_mix, w_in, lam_re, lam_im, log_dt, b_re, b_im, c_re, c_im, d_skip,
              w_glu, b_glu, conv_w, conv_b, ln_g, ln_b, w_out, g_post_mix, g_pre_ffn,
              w_up, ffn_conv_w, ffn_conv_b, w_down, g_post_ffn):
    W = dict(g_pre_mix=g_pre_mix, w_in=w_in, lam_re=lam_re, lam_im=lam_im, log_dt=log_dt,
             b_re=b_re, b_im=b_im, c_re=c_re, c_im=c_im, d_skip=d_skip, w_glu=w_glu,
             b_glu=b_glu, conv_w=conv_w, conv_b=conv_b, ln_g=ln_g, ln_b=ln_b, w_out=w_out,
             g_post_mix=g_post_mix, g_pre_ffn=g_pre_ffn, w_up=w_up, ffn_conv_w=ffn_conv_w,
             ffn_conv_b=ffn_conv_b, w_down=w_down, g_post_ffn=g_post_ffn)
    dt = x_prompt.dtype
    p_re0 = jnp.zeros((DEPTH, BATCH, S5_GROUPS, S5_STATE), jnp.float32)
    p_conv0 = jnp.zeros((DEPTH, BATCH, CONV_K - 1, CONV_CH), dt)
    p_ffn0 = jnp.zeros((DEPTH, BATCH, FFN_K - 1, 2 * D_FF), dt)
    y_prompt, p_re, p_im, p_conv, p_ffn = run_trunk(x_prompt, p_re0, p_re0, p_conv0, p_ffn0, W)
    y_sample, s_re, s_im, s_conv, s_ffn = run_trunk(x_sample, state_ssm_re, state_ssm_im,
                                                    state_conv, state_ffn_conv, W)
    return (y_prompt, y_sample, p_re, p_im, p_conv, p_ffn, s_re, s_im, s_conv, s_ffn)
```

```python
import functools
import math

import jax
import jax.numpy as jnp
from jax import lax
from jax.experimental import pallas as pl
from jax.experimental.pallas import tpu as pltpu

D_MODEL = 1024
DEPTH = 2
S5_WIDTH = 512
S5_GROUP = 16
S5_GROUPS = 32
S5_STATE = 64
S5_NSTATE = S5_GROUPS * S5_STATE
CONV_CH = 512
CONV_K = 31
MIX_IN = S5_WIDTH + 2 * CONV_CH
D_FF = 2816
FFN_K = 3
EPS = 1e-6

SUBLANES = 8
S5_BLOCKS = 4
S5_BLOCK_IN = S5_WIDTH // S5_BLOCKS
S5_BLOCK_STATE = S5_NSTATE // S5_BLOCKS
FF_BLOCK = 256
FF_NBLOCKS = D_FF // FF_BLOCK
VMEM_LIMIT_BYTES = 56 * 1024 * 1024

_F32 = jnp.float32
_BF16 = jnp.bfloat16


def _s5_prep_kernel(lr_ref, li_ref, ldt_ref, br_ref, bi_ref, ar_ref, ai_ref, bbr_ref, bbi_ref):
    dt = jnp.exp(ldt_ref[...])
    lr = lr_ref[...]
    li = li_ref[...]
    mag = jnp.exp(lr * dt)
    ar = mag * jnp.cos(li * dt)
    ai = mag * jnp.sin(li * dt)
    den = lr * lr + li * li
    cr = ((ar - 1.0) * lr + ai * li) / den
    ci = (ai * lr - (ar - 1.0) * li) / den
    br = br_ref[...]
    bi = bi_ref[...]
    ar_ref[...] = ar
    ai_ref[...] = ai
    bbr_ref[...] = cr * br - ci * bi
    bbi_ref[...] = cr * bi + ci * br


def _s5_prep(lam_re, lam_im, log_dt, b_re, b_im):
    n = DEPTH * S5_NSTATE
    lr = lam_re.reshape(n, 1)
    li = lam_im.reshape(n, 1)
    ldt = jnp.broadcast_to(log_dt[:, :, None], (DEPTH, S5_GROUPS, S5_STATE)).reshape(n, 1)
    br = b_re.reshape(n, S5_GROUP)
    bi = b_im.reshape(n, S5_GROUP)
    col = jax.ShapeDtypeStruct((n, 1), _F32)
    mat = jax.ShapeDtypeStruct((n, S5_GROUP), _F32)
    return pl.pallas_call(_s5_prep_kernel, out_shape=(col, col, mat, mat), name="s5_prep")(lr, li, ldt, br, bi)


def _rows_loop(n_rows, tile, fn):
    def body(i, carry):
        fn(pl.multiple_of(i * tile, tile))
        return carry
    lax.fori_loop(0, n_rows // tile, body, 0)


def _layer_kernel(nb, tc, multi_step,
                  x_ref, hre0_ref, him0_ref, conv0_ref, ffn0_ref,
                  g_pre_mix_ref, w_in_ref, ar_ref, ai_ref, bbar_ref, c_re_ref, c_im_ref, d_skip_ref,
                  w_glu_ref, b_glu_ref, conv_w_ref, conv_b_ref, ln_g_ref, ln_b_ref, w_out_ref,
                  g_post_mix_ref, g_pre_ffn_ref, w_up_ref, fcw_ref, fcb_ref, w_down_ref, g_post_ffn_ref,
                  y_ref, hre_out_ref, him_out_ref, conv_out_ref, ffn_out_ref,
                  hre_s, him_s, zext_s, tail_s, hb_s, p_s, xs_s, y_s, mixin_s, x1_s, eg_s, ev_s, act_s, f_s):
    m = nb * tc
    hist = (CONV_K - 1) * nb
    ftail = (FFN_K - 1) * nb
    step = pl.program_id(0)
    last = pl.num_programs(0) - 1

    @pl.when(step == 0)
    def _load_state():
        pltpu.sync_copy(hre0_ref, hre_s)
        pltpu.sync_copy(him0_ref, him_s)
        pltpu.sync_copy(conv0_ref, zext_s.at[0:hist, :])
        pltpu.sync_copy(ffn0_ref, tail_s)

    def rms_to_bf16(src_ref, g_ref):
        g = g_ref[...]
        def tile_fn(r0):
            x = src_ref[pl.ds(r0, 32), :]
            ms = jnp.mean(x * x, axis=-1, keepdims=True)
            hb_s[pl.ds(r0, 32), :] = (x * lax.rsqrt(ms + EPS) * g).astype(_BF16)
        _rows_loop(m, 32, tile_fn)

    rms_to_bf16(x_ref, g_pre_mix_ref)

    p_s[...] = jnp.dot(hb_s[...], w_in_ref[...], preferred_element_type=_F32)

    def glu_fn(r0):
        v = p_s[pl.ds(r0, 64), S5_WIDTH:S5_WIDTH + CONV_CH]
        g = p_s[pl.ds(r0, 64), S5_WIDTH + CONV_CH:MIX_IN]
        zext_s[pl.ds(hist + r0, 64), :] = v * jax.nn.sigmoid(g)
    _rows_loop(m, 64, glu_fn)

    conv_b = conv_b_ref[...]
    ln_g = ln_g_ref[...]
    ln_b = ln_b_ref[...]

    def conv_fn(r0):
        acc = jnp.zeros((64, CONV_CH), _F32)
        for k in range(CONV_K):
            acc = acc + zext_s[pl.ds(r0 + k * nb, 64), :] * conv_w_ref[k:k + 1, :]
        cpre = acc + conv_b
        mu = jnp.mean(cpre, axis=-1, keepdims=True)
        xc = cpre - mu
        ln = xc * lax.rsqrt(jnp.mean(xc * xc, axis=-1, keepdims=True) + EPS) * ln_g + ln_b
        mixin_s[pl.ds(r0, 64), S5_WIDTH:] = (ln * jax.nn.sigmoid(ln)).astype(_BF16)
    _rows_loop(m, 64, conv_fn)

    for j in range(S5_BLOCKS):
        lo = j * S5_BLOCK_STATE
        ub = p_s[:, j * S5_BLOCK_IN:(j + 1) * S5_BLOCK_IN].astype(_BF16)
        xs_s[...] = jnp.dot(ub, bbar_ref[j], preferred_element_type=_F32)
        arb = jnp.broadcast_to(ar_ref[:, lo:lo + S5_BLOCK_STATE], (SUBLANES, S5_BLOCK_STATE))
        aib = jnp.broadcast_to(ai_ref[:, lo:lo + S5_BLOCK_STATE], (SUBLANES, S5_BLOCK_STATE))
        for bt in range(nb // SUBLANES):
            b0 = bt * SUBLANES

            def scan_body(t, carry, b0=b0, arb=arb, aib=aib):
                hr, hi = carry
                r0 = pl.multiple_of(t * nb + b0, SUBLANES)
                xr = xs_s[pl.ds(r0, SUBLANES), 0:S5_BLOCK_STATE]
                xi = xs_s[pl.ds(r0, SUBLANES), S5_BLOCK_STATE:]
                hr_n = arb * hr - aib * hi + xr
                hi_n = arb * hi + aib * hr + xi
                xs_s[pl.ds(r0, SUBLANES), 0:S5_BLOCK_STATE] = hr_n
                xs_s[pl.ds(r0, SUBLANES), S5_BLOCK_STATE:] = hi_n
                return hr_n, hi_n

            init = (hre_s[b0:b0 + SUBLANES, lo:lo + S5_BLOCK_STATE], him_s[b0:b0 + SUBLANES, lo:lo + S5_BLOCK_STATE])
            hr_f, hi_f = lax.fori_loop(0, tc, scan_body, init)
            hre_s[b0:b0 + SUBLANES, lo:lo + S5_BLOCK_STATE] = hr_f
            him_s[b0:b0 + SUBLANES, lo:lo + S5_BLOCK_STATE] = hi_f
        hrb = xs_s[:, 0:S5_BLOCK_STATE].astype(_BF16)
        hib = xs_s[:, S5_BLOCK_STATE:].astype(_BF16)
        y_s[:, j * S5_BLOCK_IN:(j + 1) * S5_BLOCK_IN] = (
            jnp.dot(hrb, c_re_ref[j], preferred_element_type=_F32)
            - jnp.dot(hib, c_im_ref[j], preferred_element_type=_F32))

    d_skip = d_skip_ref[...]
    sqrt_half = math.sqrt(0.5)

    def gelu_fn(r0):
        yv = y_s[pl.ds(r0, 64), :] + d_skip * p_s[pl.ds(r0, 64), 0:S5_WIDTH]
        a = 0.5 * yv * (1.0 + lax.erf(yv * sqrt_half))
        y_s[pl.ds(r0, 64), :] = a
        mixin_s[pl.ds(r0, 64), 0:S5_WIDTH] = a.astype(_BF16)
    _rows_loop(m, 64, gelu_fn)

    p_s[:, 0:S5_WIDTH] = jnp.dot(mixin_s[:, 0:S5_WIDTH], w_glu_ref[...], preferred_element_type=_F32)
    b_glu = b_glu_ref[...]

    def gate_fn(r0):
        gate = jax.nn.sigmoid(p_s[pl.ds(r0, 64), 0:S5_WIDTH] + b_glu)
        mixin_s[pl.ds(r0, 64), 0:S5_WIDTH] = (y_s[pl.ds(r0, 64), :] * gate).astype(_BF16)
    _rows_loop(m, 64, gate_fn)

    f_s[...] = jnp.dot(mixin_s[...], w_out_ref[...], preferred_element_type=_F32)
    g_post_mix = g_post_mix_ref[...]

    def resid1_fn(r0):
        mix = f_s[pl.ds(r0, 32), :]
        ms = jnp.mean(mix * mix, axis=-1, keepdims=True)
        x1_s[pl.ds(r0, 32), :] = x_ref[pl.ds(r0, 32), :] + mix * lax.rsqrt(ms + EPS) * g_post_mix
    _rows_loop(m, 32, resid1_fn)

    rms_to_bf16(x1_s, g_pre_ffn_ref)

    for j in range(FF_NBLOCKS):
        gc = j * FF_BLOCK
        vc = D_FF + j * FF_BLOCK
        hb = hb_s[...]
        eg_s[ftail:, :] = jnp.dot(hb, w_up_ref[:, gc:gc + FF_BLOCK], preferred_element_type=_F32)
        ev_s[ftail:, :] = jnp.dot(hb, w_up_ref[:, vc:vc + FF_BLOCK], preferred_element_type=_F32)
        eg_s[0:ftail, :] = tail_s[:, gc:gc + FF_BLOCK]
        ev_s[0:ftail, :] = tail_s[:, vc:vc + FF_BLOCK]
        wg = fcw_ref[:, gc:gc + FF_BLOCK]
        wv = fcw_ref[:, vc:vc + FF_BLOCK]
        bg = fcb_ref[:, gc:gc + FF_BLOCK]
        bv = fcb_ref[:, vc:vc + FF_BLOCK]

        def ffn_fn(r0, wg=wg, wv=wv, bg=bg, bv=bv):
            cg = bg
            cv = bv
            for k in range(FFN_K):
                cg = cg + eg_s[pl.ds(r0 + k * nb, 64), :] * wg[k:k + 1, :]
                cv = cv + ev_s[pl.ds(r0 + k * nb, 64), :] * wv[k:k + 1, :]
            act_s[pl.ds(r0, 64), :] = (cg * jax.nn.sigmoid(cg) * cv).astype(_BF16)
        _rows_loop(m, 64, ffn_fn)

        tail_s[:, gc:gc + FF_BLOCK] = eg_s[m:m + ftail, :]
        tail_s[:, vc:vc + FF_BLOCK] = ev_s[m:m + ftail, :]
        part = jnp.dot(act_s[...], w_down_ref[gc:gc + FF_BLOCK, :], preferred_element_type=_F32)
        if j == 0:
            f_s[...] = part
        else:
            f_s[...] += part

    g_post_ffn = g_post_ffn_ref[...]

    def resid2_fn(r0):
        f = f_s[pl.ds(r0, 32), :]
        ms = jnp.mean(f * f, axis=-1, keepdims=True)
        y_ref[pl.ds(r0, 32), :] = x1_s[pl.ds(r0, 32), :] + f * lax.rsqrt(ms + EPS) * g_post_ffn
    _rows_loop(m, 32, resid2_fn)

    @pl.when(step == last)
    def _emit_state():
        pltpu.sync_copy(hre_s, hre_out_ref)
        pltpu.sync_copy(him_s, him_out_ref)
        pltpu.sync_copy(zext_s.at[m:m + hist, :], conv_out_ref)
        pltpu.sync_copy(tail_s, ffn_out_ref)

    if multi_step:
        zext_s[0:hist, :] = zext_s[m:m + hist, :]


def _in_hbm():
    return pl.BlockSpec(memory_space=pl.ANY)


def _resident():
    return pl.BlockSpec(memory_space=pltpu.MemorySpace.VMEM)


def _run_layer(nb, tc, x2d, hre0, him0, conv0, ffn0, w):
    rows = x2d.shape[0]
    m = nb * tc
    assert rows % m == 0 and m % 64 == 0 and nb % SUBLANES == 0
    n_steps = rows // m
    multi_step = n_steps > 1
    hist = (CONV_K - 1) * nb
    ftail = (FFN_K - 1) * nb
    assert (not multi_step) or m >= hist

    in_specs = ([pl.BlockSpec((m, D_MODEL), lambda c: (c, 0))]
                + [_in_hbm() for _ in range(4)] + [_resident() for _ in range(22)])
    out_specs = (pl.BlockSpec((m, D_MODEL), lambda c: (c, 0)),) + tuple(_in_hbm() for _ in range(4))
    out_shape = (
        jax.ShapeDtypeStruct((rows, D_MODEL), _F32),
        jax.ShapeDtypeStruct((nb, S5_NSTATE), _F32), jax.ShapeDtypeStruct((nb, S5_NSTATE), _F32),
        jax.ShapeDtypeStruct((hist, CONV_CH), _F32), jax.ShapeDtypeStruct((ftail, 2 * D_FF), _F32),
    )
    scratch = [
        pltpu.VMEM((nb, S5_NSTATE), _F32),
        pltpu.VMEM((nb, S5_NSTATE), _F32),
        pltpu.VMEM((hist + m, CONV_CH), _F32),
        pltpu.VMEM((ftail, 2 * D_FF), _F32),
        pltpu.VMEM((m, D_MODEL), _BF16),
        pltpu.VMEM((m, MIX_IN), _F32),
        pltpu.VMEM((m, 2 * S5_BLOCK_STATE), _F32),
        pltpu.VMEM((m, S5_WIDTH), _F32),
        pltpu.VMEM((m, D_MODEL), _BF16),
        pltpu.VMEM((m, D_MODEL), _F32),
        pltpu.VMEM((ftail + m, FF_BLOCK), _F32),
        pltpu.VMEM((ftail + m, FF_BLOCK), _F32),
        pltpu.VMEM((m, FF_BLOCK), _BF16),
        pltpu.VMEM((m, D_MODEL), _F32),
    ]
    kern = functools.partial(_layer_kernel, nb, tc, multi_step)
    return pl.pallas_call(
        kern,
        grid=(n_steps,),
        in_specs=in_specs,
        out_specs=out_specs,
        out_shape=out_shape,
        scratch_shapes=scratch,
        compiler_params=pltpu.CompilerParams(
            dimension_semantics=("arbitrary",), vmem_limit_bytes=VMEM_LIMIT_BYTES),
        name=f"trunk_layer_nb{nb}",
    )(x2d, hre0, him0, conv0, ffn0, *w)


def _block_diag_in(bb):
    t = bb.reshape(S5_BLOCKS, 8, S5_STATE, S5_GROUP).transpose(0, 1, 3, 2)
    eye = jnp.eye(8, dtype=bb.dtype)
    full = t[:, :, :, None, :] * eye[None, :, None, :, None]
    return full.reshape(S5_BLOCKS, 8 * S5_GROUP, 8 * S5_STATE)


def _block_diag_out(c):
    t = c.reshape(S5_BLOCKS, 8, S5_GROUP, S5_STATE).transpose(0, 1, 3, 2)
    eye = jnp.eye(8, dtype=c.dtype)
    full = t[:, :, :, None, :] * eye[None, :, None, :, None]
    return full.reshape(S5_BLOCKS, 8 * S5_STATE, 8 * S5_GROUP)


def kernel(x_prompt, x_sample, state_ssm_re, state_ssm_im, state_conv, state_ffn_conv, g_pre_mix, w_in, lam_re, lam_im, log_dt, b_re, b_im, c_re, c_im, d_skip, w_glu, b_glu, conv_w, conv_b, ln_g, ln_b, w_out, g_post_mix, g_pre_ffn, w_up, ffn_conv_w, ffn_conv_b, w_down, g_post_ffn):
    batch, seq, _ = x_prompt.shape
    dec_batch = x_sample.shape[0]

    ar, ai, bbr, bbi = _s5_prep(lam_re, lam_im, log_dt, b_re, b_im)
    ar = ar.reshape(DEPTH, 1, S5_NSTATE)
    ai = ai.reshape(DEPTH, 1, S5_NSTATE)
    bbr = bbr.reshape(DEPTH, S5_GROUPS, S5_STATE, S5_GROUP)
    bbi = bbi.reshape(DEPTH, S5_GROUPS, S5_STATE, S5_GROUP)

    def layer_weights(l):
        row = lambda a: a[l].reshape(1, -1)
        bbar = jnp.concatenate([_block_diag_in(bbr[l]), _block_diag_in(bbi[l])], axis=-1).astype(_BF16)
        return (
            row(g_pre_mix), w_in[l].astype(_BF16), ar[l], ai[l], bbar,
            _block_diag_out(c_re[l]).astype(_BF16), _block_diag_out(c_im[l]).astype(_BF16), row(d_skip),
            w_glu[l].astype(_BF16), row(b_glu), conv_w[l], row(conv_b), row(ln_g), row(ln_b),
            w_out[l].astype(_BF16), row(g_post_mix), row(g_pre_ffn), w_up[l].astype(_BF16),
            ffn_conv_w[l], row(ffn_conv_b), w_down[l].astype(_BF16), row(g_post_ffn),
        )

    weights = [layer_weights(l) for l in range(DEPTH)]

    def run_trunk(x, ssm_re, ssm_im, conv_buf, ffn_buf, tc):
        nb, length, _ = x.shape
        x2d = x.transpose(1, 0, 2).reshape(length * nb, D_MODEL)
        new = []
        for l in range(DEPTH):
            hre0 = ssm_re[l].reshape(nb, S5_NSTATE)
            him0 = ssm_im[l].reshape(nb, S5_NSTATE)
            conv0 = conv_buf[l].transpose(1, 0, 2).reshape((CONV_K - 1) * nb, CONV_CH)
            ffn0 = ffn_buf[l].transpose(1, 0, 2).reshape((FFN_K - 1) * nb, 2 * D_FF)
            x2d, hre, him, conv_n, ffn_n = _run_layer(nb, tc, x2d, hre0, him0, conv0, ffn0, weights[l])
            new.append((
                hre.reshape(nb, S5_GROUPS, S5_STATE), him.reshape(nb, S5_GROUPS, S5_STATE),
                conv_n.reshape(CONV_K - 1, nb, CONV_CH).transpose(1, 0, 2),
                ffn_n.reshape(FFN_K - 1, nb, 2 * D_FF).transpose(1, 0, 2)))
        y = x2d.reshape(length, nb, D_MODEL).transpose(1, 0, 2)
        return (y,) + tuple(jnp.stack([n[i] for n in new]) for i in range(4))

    dt = x_prompt.dtype
    zeros_ssm = jnp.zeros((DEPTH, batch, S5_GROUPS, S5_STATE), _F32)
    zeros_conv = jnp.zeros((DEPTH, batch, CONV_K - 1, CONV_CH), dt)
    zeros_ffn = jnp.zeros((DEPTH, batch, FFN_K - 1, 2 * D_FF), dt)
    y_p, p_re, p_im, p_conv, p_ffn = run_trunk(x_prompt, zeros_ssm, zeros_ssm, zeros_conv, zeros_ffn, tc=32)
    y_s, s_re, s_im, s_conv, s_ffn = run_trunk(x_sample, state_ssm_re, state_ssm_im, state_conv, state_ffn_conv, tc=1)
    return (y_p, y_s, p_re, p_im, p_conv, p_ffn, s_re, s_im, s_conv, s_ffn)
```

```python
import functools
import math

import jax
import jax.numpy as jnp
from jax import lax
from jax.experimental import pallas as pl
from jax.experimental.pallas import tpu as pltpu

D_MODEL = 1024
DEPTH = 2
S5_WIDTH = 512
S5_GROUP = 16
S5_GROUPS = 32
S5_STATE = 64
S5_NSTATE = S5_GROUPS * S5_STATE
CONV_CH = 512
CONV_K = 31
MIX_IN = S5_WIDTH + 2 * CONV_CH
D_FF = 2816
FFN_K = 3
EPS = 1e-6

SUB = 8
PACK = 16
S5_BLOCKS = 4
S5_BLOCK_IN = S5_WIDTH // S5_BLOCKS
S5_BLOCK_STATE = S5_NSTATE // S5_BLOCKS
FF_BLOCK = 256
FF_NBLOCKS = D_FF // FF_BLOCK
VMEM_LIMIT_BYTES = 62 * 1024 * 1024

_F32 = jnp.float32
_BF16 = jnp.bfloat16


def _s5_prep_kernel(lr_ref, li_ref, ldt_ref, br_ref, bi_ref, ar_ref, ai_ref, bbr_ref, bbi_ref):
    dt = jnp.exp(ldt_ref[...])
    lr = lr_ref[...]
    li = li_ref[...]
    mag = jnp.exp(lr * dt)
    ar = mag * jnp.cos(li * dt)
    ai = mag * jnp.sin(li * dt)
    den = lr * lr + li * li
    cr = ((ar - 1.0) * lr + ai * li) / den
    ci = (ai * lr - (ar - 1.0) * li) / den
    br = br_ref[...]
    bi = bi_ref[...]
    ar_ref[...] = ar
    ai_ref[...] = ai
    bbr_ref[...] = cr * br - ci * bi
    bbi_ref[...] = cr * bi + ci * br


def _s5_prep(lam_re, lam_im, log_dt, b_re, b_im):
    n = DEPTH * S5_NSTATE
    lr = lam_re.reshape(n, 1)
    li = lam_im.reshape(n, 1)
    ldt = jnp.broadcast_to(log_dt[:, :, None], (DEPTH, S5_GROUPS, S5_STATE)).reshape(n, 1)
    br = b_re.reshape(n, S5_GROUP)
    bi = b_im.reshape(n, S5_GROUP)
    col = jax.ShapeDtypeStruct((n, 1), _F32)
    mat = jax.ShapeDtypeStruct((n, S5_GROUP), _F32)
    return pl.pallas_call(_s5_prep_kernel, out_shape=(col, col, mat, mat), name="s5_prep")(lr, li, ldt, br, bi)


class _Task:
    def __init__(self, name, unit, cost, deps, fn):
        self.name, self.unit, self.cost, self.deps, self.fn = name, unit, max(1, cost), deps, fn


def _list_schedule(tasks):
    succ = {t.name: [] for t in tasks}
    for t in tasks:
        for d in t.deps:
            succ[d].append(t.name)
    prio = {}
    for t in reversed(tasks):
        prio[t.name] = t.cost + max([prio[s] for s in succ[t.name]], default=0)
    free = {"M": 0, "V": 0}
    finish, order, pending = {}, [], list(tasks)
    while pending:
        best = None
        for t in pending:
            if all(d in finish for d in t.deps):
                start = max([finish[d] for d in t.deps] + [free[t.unit]])
                key = (start, -prio[t.name])
                if best is None or key < best[0]:
                    best = (key, t)
        (start, _), t = best
        finish[t.name] = free[t.unit] = start + t.cost
        order.append(t)
        pending.remove(t)
    return order


def _sigmoid(x):
    return 1.0 / (1.0 + jnp.exp(-x))


def _rms_scale(x):
    return x * lax.rsqrt(jnp.mean(x * x, axis=-1, keepdims=True) + EPS)


def _layer_kernel(nb, tc, n_chunks, pipelined,
                  x_ref, hre0_ref, him0_ref, conv0_ref, ffn0_ref,
                  g_pre_mix_ref, w_in_ref, ar_ref, ai_ref, bbar_ref, c_re_ref, c_im_ref, d_skip_ref,
                  w_glu_ref, b_glu_ref, conv_w_ref, conv_b_ref, ln_g_ref, ln_b_ref, w_out_ref,
                  g_post_mix_ref, g_pre_ffn_ref, w_up_ref, fcw_ref, fcb_ref, w_down_ref, g_post_ffn_ref,
                  y_ref, hre_out_ref, him_out_ref, conv_out_ref, ffn_out_ref,
                  hre_s, him_s, zext_s, tail_s,
                  par1k_s, par512_s, convw_s, fcw_s, assm_s,
                  hba_s, hbb_s, p_s, xs_s, hs_s, y_s, mixin_s, x1_s, x1b_s, e_s, act_s, f_s):
    m = nb * tc
    nbt = nb // SUB
    hist = (CONV_K - 1) * nb
    ftail = (FFN_K - 1) * nb
    multi_chunk = n_chunks > 1
    step = pl.program_id(0)
    last = pl.num_programs(0) - 1
    a_valid = (step < n_chunks) if pipelined else None
    b_valid = (step >= 1) if pipelined else None

    def commit(ref, idx, new, valid):
        ref[idx] = new if valid is None else jnp.where(valid, new, ref[idx])

    @pl.when(step == 0)
    def _prologue():
        pltpu.sync_copy(hre0_ref, hre_s)
        pltpu.sync_copy(him0_ref, him_s)
        pltpu.sync_copy(conv0_ref, zext_s.at[0:hist, :])
        pltpu.sync_copy(ffn0_ref, tail_s)

        def rep(src_row):
            return jnp.broadcast_to(src_row, (SUB, src_row.shape[1]))
        for i, r in enumerate((g_pre_mix_ref, g_post_mix_ref, g_pre_ffn_ref, g_post_ffn_ref)):
            par1k_s[SUB * i:SUB * (i + 1), :] = rep(r[...])
        for i, r in enumerate((conv_b_ref, ln_g_ref, ln_b_ref, d_skip_ref, b_glu_ref)):
            par512_s[SUB * i:SUB * (i + 1), :] = rep(r[...])
        for k in range(CONV_K):
            convw_s[SUB * k:SUB * (k + 1), :] = rep(conv_w_ref[k:k + 1, :])
        for k in range(FFN_K):
            fcw_s[SUB * k:SUB * (k + 1), :] = rep(fcw_ref[k:k + 1, :])
        fcw_s[SUB * FFN_K:SUB * (FFN_K + 1), :] = rep(fcb_ref[...])
        assm_s[0:SUB, :] = rep(ar_ref[...])
        assm_s[SUB:2 * SUB, :] = rep(ai_ref[...])
        if pipelined:
            x1_s[...] = jnp.zeros_like(x1_s)

    def par1k(i):
        return par1k_s[SUB * i:SUB * (i + 1), :]

    def par512(i):
        return par512_s[SUB * i:SUB * (i + 1), :]

    def pair_to_bf16(fn, r):
        return jnp.concatenate([fn(r), fn(r + SUB)], axis=0).astype(_BF16)

    def row_parts(n_parts):
        n_parts = max(1, min(n_parts, m // PACK))
        tiles = m // PACK
        edges = [PACK * ((tiles * i) // n_parts) for i in range(n_parts + 1)]
        return [(edges[i], edges[i + 1]) for i in range(n_parts)]

    tasks = []

    def task(name, unit, cost, deps, fn):
        tasks.append(_Task(name, unit, cost * m // 512, [d for d in deps if d is not None], fn))
        return name

    def staged(prefix, n_parts, cost, deps, fn):
        parts = row_parts(n_parts)
        return [task(f"{prefix}.{i}", "V", cost // len(parts), deps, functools.partial(fn, r0, r1))
                for i, (r0, r1) in enumerate(parts)]

    def ffn_norm(r0, r1):
        g = par1k(2)

        def half(r8):
            x1 = x1_s[r8:r8 + SUB, :]
            x1b_s[r8:r8 + SUB, :] = x1
            return _rms_scale(x1) * g
        for r in range(r0, r1, PACK):
            hbb_s[r:r + PACK, :] = pair_to_bf16(half, r)

    def pre_norm(r0, r1):
        g = par1k(0)
        for r in range(r0, r1, PACK):
            hba_s[r:r + PACK, :] = pair_to_bf16(lambda r8: _rms_scale(x_ref[r8:r8 + SUB, :]) * g, r)

    def in_proj():
        p_s[...] = jnp.dot(hba_s[...], w_in_ref[...], preferred_element_type=_F32)

    def conv_glu(r0, r1):
        for r8 in range(r0, r1, SUB):
            v = p_s[r8:r8 + SUB, S5_WIDTH:S5_WIDTH + CONV_CH]
            gt = p_s[r8:r8 + SUB, S5_WIDTH + CONV_CH:MIX_IN]
            zext_s[hist + r8:hist + r8 + SUB, :] = v * _sigmoid(gt)

    def conv_ln(r0, r1):
        conv_b, ln_g, ln_b = par512(0), par512(1), par512(2)

        def half(r8):
            acc = zext_s[r8:r8 + SUB, :] * convw_s[0:SUB, :]
            for k in range(1, CONV_K):
                acc = acc + zext_s[r8 + k * nb:r8 + k * nb + SUB, :] * convw_s[SUB * k:SUB * (k + 1), :]
            cpre = acc + conv_b
            xc = cpre - jnp.mean(cpre, axis=-1, keepdims=True)
            ln = xc * lax.rsqrt(jnp.mean(xc * xc, axis=-1, keepdims=True) + EPS) * ln_g + ln_b
            return ln * _sigmoid(ln)
        for r in range(r0, r1, PACK):
            mixin_s[r:r + PACK, S5_WIDTH:] = pair_to_bf16(half, r)

    def conv_carry():
        commit(zext_s, (slice(0, hist), slice(None)), zext_s[m:m + hist, :], a_valid)

    def s5_in(j):
        ub = p_s[:, j * S5_BLOCK_IN:(j + 1) * S5_BLOCK_IN].astype(_BF16)
        xs_s[...] = jnp.dot(ub, bbar_ref[j], preferred_element_type=_F32)

    def s5_scan(j):
        cols = slice(j * S5_BLOCK_STATE, (j + 1) * S5_BLOCK_STATE)
        arb = assm_s[0:SUB, cols]
        aib = assm_s[SUB:2 * SUB, cols]
        state = [(hre_s[SUB * bt:SUB * (bt + 1), cols], him_s[SUB * bt:SUB * (bt + 1), cols])
                 for bt in range(nbt)]
        for r in range(0, m, PACK):
            new_r, new_i = [], []
            for r8 in (r, r + SUB):
                bt = (r8 // SUB) % nbt
                hr, hi = state[bt]
                xr = xs_s[r8:r8 + SUB, 0:S5_BLOCK_STATE]
                xi = xs_s[r8:r8 + SUB, S5_BLOCK_STATE:]
                hr_n = arb * hr - aib * hi + xr
                hi_n = arb * hi + aib * hr + xi
                state[bt] = (hr_n, hi_n)
                new_r.append(hr_n)
                new_i.append(hi_n)
            hs_s[j % 2, r:r + PACK, 0:S5_BLOCK_STATE] = jnp.concatenate(new_r, axis=0).astype(_BF16)
            hs_s[j % 2, r:r + PACK, S5_BLOCK_STATE:] = jnp.concatenate(new_i, axis=0).astype(_BF16)
        for bt in range(nbt):
            rows = slice(SUB * bt, SUB * (bt + 1))
            commit(hre_s, (rows, cols), state[bt][0], a_valid)
            commit(him_s, (rows, cols), state[bt][1], a_valid)

    def s5_out(j):
        y_s[:, j * S5_BLOCK_IN:(j + 1) * S5_BLOCK_IN] = (
            jnp.dot(hs_s[j % 2, :, 0:S5_BLOCK_STATE], c_re_ref[j], preferred_element_type=_F32)
            - jnp.dot(hs_s[j % 2, :, S5_BLOCK_STATE:], c_im_ref[j], preferred_element_type=_F32))

    def gelu(r0, r1):
        d_skip = par512(3)
        sqrt_half = math.sqrt(0.5)

        def half(r8):
            yv = y_s[r8:r8 + SUB, :] + d_skip * p_s[r8:r8 + SUB, 0:S5_WIDTH]
            a = 0.5 * yv * (1.0 + lax.erf(yv * sqrt_half))
            y_s[r8:r8 + SUB, :] = a
            return a
        for r in range(r0, r1, PACK):
            mixin_s[r:r + PACK, 0:S5_WIDTH] = pair_to_bf16(half, r)

    def glu_proj():
        p_s[:, 0:S5_WIDTH] = jnp.dot(mixin_s[:, 0:S5_WIDTH], w_glu_ref[...], preferred_element_type=_F32)

    def glu_gate(r0, r1):
        b_glu = par512(4)
        for r in range(r0, r1, PACK):
            mixin_s[r:r + PACK, 0:S5_WIDTH] = pair_to_bf16(
                lambda r8: y_s[r8:r8 + SUB, :] * _sigmoid(p_s[r8:r8 + SUB, 0:S5_WIDTH] + b_glu), r)

    def out_proj():
        p_s[:, 0:D_MODEL] = jnp.dot(mixin_s[...], w_out_ref[...], preferred_element_type=_F32)

    def mix_residual(r0, r1):
        g2 = par1k(1)
        for r8 in range(r0, r1, SUB):
            x1_s[r8:r8 + SUB, :] = x_ref[r8:r8 + SUB, :] + _rms_scale(p_s[r8:r8 + SUB, 0:D_MODEL]) * g2

    def ff_cols(j):
        gc = j * FF_BLOCK
        vc = D_FF + j * FF_BLOCK
        jc = slice(2 * FF_BLOCK * j, 2 * FF_BLOCK * (j + 1))
        return gc, vc, jc

    def ff_up(j):
        gc, vc, jc = ff_cols(j)
        slot = j % 2
        e_s[slot, ftail:ftail + m, :] = jnp.dot(hbb_s[...], w_up_ref[:, jc], preferred_element_type=_F32)
        e_s[slot, 0:ftail, 0:FF_BLOCK] = tail_s[:, gc:gc + FF_BLOCK]
        e_s[slot, 0:ftail, FF_BLOCK:] = tail_s[:, vc:vc + FF_BLOCK]
        new_tail = e_s[slot, m:m + ftail, :]
        commit(tail_s, (slice(None), slice(gc, gc + FF_BLOCK)), new_tail[:, 0:FF_BLOCK], b_valid)
        commit(tail_s, (slice(None), slice(vc, vc + FF_BLOCK)), new_tail[:, FF_BLOCK:], b_valid)

    def ff_act(j, r0, r1):
        _, _, jc = ff_cols(j)
        slot = j % 2
        w0 = fcw_s[0:SUB, jc]
        w1 = fcw_s[SUB:2 * SUB, jc]
        w2 = fcw_s[2 * SUB:3 * SUB, jc]
        bias = fcw_s[3 * SUB:4 * SUB, jc]

        def half(r8):
            c = (e_s[slot, r8:r8 + SUB, :] * w0
                 + e_s[slot, r8 + nb:r8 + nb + SUB, :] * w1
                 + e_s[slot, r8 + 2 * nb:r8 + 2 * nb + SUB, :] * w2) + bias
            cg = c[:, 0:FF_BLOCK]
            return cg * _sigmoid(cg) * c[:, FF_BLOCK:]
        for r in range(r0, r1, PACK):
            act_s[slot, r:r + PACK, :] = pair_to_bf16(half, r)

    def ff_down(j):
        gc, _, _ = ff_cols(j)
        part = jnp.dot(act_s[j % 2], w_down_ref[gc:gc + FF_BLOCK, :], preferred_element_type=_F32)
        if j == 0:
            f_s[...] = part
        else:
            f_s[...] += part

    def ffn_residual(r0, r1):
        g4 = par1k(3)
        for r8 in range(r0, r1, SUB):
            y_ref[r8:r8 + SUB, :] = x1b_s[r8:r8 + SUB, :] + _rms_scale(f_s[r8:r8 + SUB, :]) * g4

    t_pre = staged("pre_norm", 2, 600, [], pre_norm)
    t_in = task("in_proj", "M", 3072, t_pre, in_proj)
    t_glu = staged("conv_glu", 1, 400, [t_in], conv_glu)
    t_conv = staged("conv_ln", 8, 5200, t_glu, conv_ln)
    if multi_chunk:
        t_conv = [task("conv_carry", "V", 60, t_conv, conv_carry)]
    t_scan, t_sout = [], []
    for j in range(S5_BLOCKS):
        t_sin = task(f"s5_in.{j}", "M", 512, [t_in] + t_scan[-1:], functools.partial(s5_in, j))
        t_scan.append(task(f"s5_scan.{j}", "V", 900, [t_sin] + t_sout[-2:-1], functools.partial(s5_scan, j)))
        t_sout.append(task(f"s5_out.{j}", "M", 512, [t_scan[-1]], functools.partial(s5_out, j)))
    t_gelu = staged("gelu", 2, 800, t_sout, gelu)
    t_gproj = task("glu_proj", "M", 512, t_gelu, glu_proj)
    t_gate = staged("glu_gate", 1, 400, [t_gproj], glu_gate)
    t_out = task("out_proj", "M", 2048, t_gate + t_conv, out_proj)
    if pipelined:
        t_fnorm = staged("ffn_norm", 2, 600, [], ffn_norm)
        t_res = staged("mix_residual", 2, 800, [t_out] + t_fnorm, mix_residual)
    else:
        t_res = staged("mix_residual", 2, 800, [t_out], mix_residual)
        t_fnorm = staged("ffn_norm", 2, 600, t_res, ffn_norm)
    t_act, t_down = [], []
    for j in range(FF_NBLOCKS):
        t_up = task(f"ff_up.{j}", "M", 1024, t_fnorm + (t_act[-2] if j >= 2 else []), functools.partial(ff_up, j))
        t_act.append(staged(f"ff_act.{j}", 2, 800, [t_up] + t_down[-2:-1], functools.partial(ff_act, j)))
        t_down.append(task(f"ff_down.{j}", "M", 512, t_act[-1] + t_down[-1:], functools.partial(ff_down, j)))
    staged("ffn_residual", 2, 600, t_down[-1:], ffn_residual)

    for t in _list_schedule(tasks):
        t.fn()

    @pl.when(step == last)
    def _epilogue():
        pltpu.sync_copy(hre_s, hre_out_ref)
        pltpu.sync_copy(him_s, him_out_ref)
        if multi_chunk:
            pltpu.sync_copy(zext_s.at[0:hist, :], conv_out_ref)
        else:
            pltpu.sync_copy(zext_s.at[m:m + hist, :], conv_out_ref)
        pltpu.sync_copy(tail_s, ffn_out_ref)


def _in_hbm():
    return pl.BlockSpec(memory_space=pl.ANY)


def _resident():
    return pl.BlockSpec(memory_space=pltpu.MemorySpace.VMEM)


def _run_layer(nb, tc, x2d, hre0, him0, conv0, ffn0, w):
    rows = x2d.shape[0]
    m = nb * tc
    assert rows % m == 0 and m % PACK == 0 and nb % SUB == 0
    n_chunks = rows // m
    pipelined = n_chunks > 1
    hist = (CONV_K - 1) * nb
    ftail = (FFN_K - 1) * nb
    assert n_chunks == 1 or m >= hist

    if pipelined:
        n_steps = n_chunks + 1
        x_map = lambda c: (jnp.minimum(c, n_chunks - 1), 0)
        y_map = lambda c: (jnp.maximum(c - 1, 0), 0)
    else:
        n_steps = n_chunks
        x_map = y_map = lambda c: (c, 0)

    in_specs = ([pl.BlockSpec((m, D_MODEL), x_map)]
                + [_in_hbm() for _ in range(4)] + [_resident() for _ in range(22)])
    out_specs = (pl.BlockSpec((m, D_MODEL), y_map),) + tuple(_in_hbm() for _ in range(4))
    out_shape = (
        jax.ShapeDtypeStruct((rows, D_MODEL), _F32),
        jax.ShapeDtypeStruct((nb, S5_NSTATE), _F32), jax.ShapeDtypeStruct((nb, S5_NSTATE), _F32),
        jax.ShapeDtypeStruct((hist, CONV_CH), _F32), jax.ShapeDtypeStruct((ftail, 2 * D_FF), _F32),
    )
    scratch = [
        pltpu.VMEM((nb, S5_NSTATE), _F32),
        pltpu.VMEM((nb, S5_NSTATE), _F32),
        pltpu.VMEM((hist + m, CONV_CH), _F32),
        pltpu.VMEM((ftail, 2 * D_FF), _F32),
        pltpu.VMEM((4 * SUB, D_MODEL), _F32),
        pltpu.VMEM((5 * SUB, CONV_CH), _F32),
        pltpu.VMEM((CONV_K * SUB, CONV_CH), _F32),
        pltpu.VMEM(((FFN_K + 1) * SUB, 2 * D_FF), _F32),
        pltpu.VMEM((2 * SUB, S5_NSTATE), _F32),
        pltpu.VMEM((m, D_MODEL), _BF16),
        pltpu.VMEM((m, D_MODEL), _BF16),
        pltpu.VMEM((m, MIX_IN), _F32),
        pltpu.VMEM((m, 2 * S5_BLOCK_STATE), _F32),
        pltpu.VMEM((2, m, 2 * S5_BLOCK_STATE), _BF16),
        pltpu.VMEM((m, S5_WIDTH), _F32),
        pltpu.VMEM((m, D_MODEL), _BF16),
        pltpu.VMEM((m, D_MODEL), _F32),
        pltpu.VMEM((m, D_MODEL), _F32),
        pltpu.VMEM((2, ftail + m, 2 * FF_BLOCK), _F32),
        pltpu.VMEM((2, m, FF_BLOCK), _BF16),
        pltpu.VMEM((m, D_MODEL), _F32),
    ]
    kern = functools.partial(_layer_kernel, nb, tc, n_chunks, pipelined)
    return pl.pallas_call(
        kern,
        grid=(n_steps,),
        in_specs=in_specs,
        out_specs=out_specs,
        out_shape=out_shape,
        scratch_shapes=scratch,
        compiler_params=pltpu.CompilerParams(
            dimension_semantics=("arbitrary",), vmem_limit_bytes=VMEM_LIMIT_BYTES),
        name=f"trunk_layer_nb{nb}",
    )(x2d, hre0, him0, conv0, ffn0, *w)


def _block_diag_in(bb):
    t = bb.reshape(S5_BLOCKS, 8, S5_STATE, S5_GROUP).transpose(0, 1, 3, 2)
    eye = jnp.eye(8, dtype=bb.dtype)
    full = t[:, :, :, None, :] * eye[None, :, None, :, None]
    return full.reshape(S5_BLOCKS, 8 * S5_GROUP, 8 * S5_STATE)


def _block_diag_out(c):
    t = c.reshape(S5_BLOCKS, 8, S5_GROUP, S5_STATE).transpose(0, 1, 3, 2)
    eye = jnp.eye(8, dtype=c.dtype)
    full = t[:, :, :, None, :] * eye[None, :, None, :, None]
    return full.reshape(S5_BLOCKS, 8 * S5_STATE, 8 * S5_GROUP)


def _ff_blocked(a):
    lead = a.shape[:-1]
    return a.reshape(lead + (2, FF_NBLOCKS, FF_BLOCK)).swapaxes(-3, -2).reshape(lead + (2 * D_FF,))


def kernel(x_prompt, x_sample, state_ssm_re, state_ssm_im, state_conv, state_ffn_conv, g_pre_mix, w_in, lam_re, lam_im, log_dt, b_re, b_im, c_re, c_im, d_skip, w_glu, b_glu, conv_w, conv_b, ln_g, ln_b, w_out, g_post_mix, g_pre_ffn, w_up, ffn_conv_w, ffn_conv_b, w_down, g_post_ffn):
    batch, seq, _ = x_prompt.shape

    ar, ai, bbr, bbi = _s5_prep(lam_re, lam_im, log_dt, b_re, b_im)
    ar = ar.reshape(DEPTH, 1, S5_NSTATE)
    ai = ai.reshape(DEPTH, 1, S5_NSTATE)
    bbr = bbr.reshape(DEPTH, S5_GROUPS, S5_STATE, S5_GROUP)
    bbi = bbi.reshape(DEPTH, S5_GROUPS, S5_STATE, S5_GROUP)

    def layer_weights(l):
        row = lambda a: a[l].reshape(1, -1)
        bbar = jnp.concatenate([_block_diag_in(bbr[l]), _block_diag_in(bbi[l])], axis=-1).astype(_BF16)
        return (
            row(g_pre_mix), w_in[l].astype(_BF16), ar[l], ai[l], bbar,
            _block_diag_out(c_re[l]).astype(_BF16), _block_diag_out(c_im[l]).astype(_BF16), row(d_skip),
            w_glu[l].astype(_BF16), row(b_glu), conv_w[l], row(conv_b), row(ln_g), row(ln_b),
            w_out[l].astype(_BF16), row(g_post_mix), row(g_pre_ffn), _ff_blocked(w_up[l]).astype(_BF16),
            _ff_blocked(ffn_conv_w[l]), _ff_blocked(row(ffn_conv_b)), w_down[l].astype(_BF16), row(g_post_ffn),
        )

    weights = [layer_weights(l) for l in range(DEPTH)]

    def run_trunk(x, ssm_re, ssm_im, conv_buf, ffn_buf, tc):
        nb, length, _ = x.shape
        x2d = x.transpose(1, 0, 2).reshape(length * nb, D_MODEL)
        new = []
        for l in range(DEPTH):
            hre0 = ssm_re[l].reshape(nb, S5_NSTATE)
            him0 = ssm_im[l].reshape(nb, S5_NSTATE)
            conv0 = conv_buf[l].transpose(1, 0, 2).reshape((CONV_K - 1) * nb, CONV_CH)
            ffn0 = ffn_buf[l].transpose(1, 0, 2).reshape((FFN_K - 1) * nb, 2 * D_FF)
            x2d, hre, him, conv_n, ffn_n = _run_layer(nb, tc, x2d, hre0, him0, conv0, ffn0, weights[l])
            new.append((
                hre.reshape(nb, S5_GROUPS, S5_STATE), him.reshape(nb, S5_GROUPS, S5_STATE),
                conv_n.reshape(CONV_K - 1, nb, CONV_CH).transpose(1, 0, 2),
                ffn_n.reshape(FFN_K - 1, nb, 2 * D_FF).transpose(1, 0, 2)))
        y = x2d.reshape(length, nb, D_MODEL).transpose(1, 0, 2)
        return (y,) + tuple(jnp.stack([n[i] for n in new]) for i in range(4))

    dt = x_prompt.dtype
    zeros_ssm = jnp.zeros((DEPTH, batch, S5_GROUPS, S5_STATE), _F32)
    zeros_conv = jnp.zeros((DEPTH, batch, CONV_K - 1, CONV_CH), dt)
    zeros_ffn = jnp.zeros((DEPTH, batch, FFN_K - 1, 2 * D_FF), dt)
    y_p, p_re, p_im, p_conv, p_ffn = run_trunk(x_prompt, zeros_ssm, zeros_ssm, zeros_conv, zeros_ffn, tc=64)
    y_s, s_re, s_im, s_conv, s_ffn = run_trunk(x_sample, state_ssm_re, state_ssm_im, state_conv, state_ffn_conv, tc=1)
    return (y_p, y_s, p_re, p_im, p_conv, p_ffn, s_re, s_im, s_conv, s_ffn)
```

```python
import functools
import math

import jax
import jax.numpy as jnp
from jax import lax
from jax.experimental import pallas as pl
from jax.experimental.pallas import tpu as pltpu

D_MODEL = 1024
DEPTH = 2
S5_WIDTH = 512
S5_GROUP = 16
S5_GROUPS = 32
S5_STATE = 64
S5_NSTATE = S5_GROUPS * S5_STATE
CONV_CH = 512
CONV_K = 31
MIX_IN = S5_WIDTH + 2 * CONV_CH
D_FF = 2816
FFN_K = 3
EPS = 1e-6

SUB = 8
PACK = 16
S5_BLOCKS = 4
S5_BLOCK_IN = S5_WIDTH // S5_BLOCKS
S5_BLOCK_STATE = S5_NSTATE // S5_BLOCKS
FF_BLOCK = 256
FF_NBLOCKS = D_FF // FF_BLOCK
FF_SLOTS = 2
VMEM_LIMIT_BYTES = 62 * 1024 * 1024

_F32 = jnp.float32
_BF16 = jnp.bfloat16


def _s5_prep_kernel(lr_ref, li_ref, ldt_ref, br_ref, bi_ref, ar_ref, ai_ref, bbr_ref, bbi_ref):
    dt = jnp.exp(ldt_ref[...])
    lr = lr_ref[...]
    li = li_ref[...]
    mag = jnp.exp(lr * dt)
    ar = mag * jnp.cos(li * dt)
    ai = mag * jnp.sin(li * dt)
    den = lr * lr + li * li
    cr = ((ar - 1.0) * lr + ai * li) / den
    ci = (ai * lr - (ar - 1.0) * li) / den
    ar_ref[...] = ar
    ai_ref[...] = ai
    for l in range(DEPTH):
        rows = slice(l * S5_GROUP, (l + 1) * S5_GROUP)
        br = br_ref[rows, :]
        bi = bi_ref[rows, :]
        bbr_ref[rows, :] = cr[l:l + 1, :] * br - ci[l:l + 1, :] * bi
        bbi_ref[rows, :] = cr[l:l + 1, :] * bi + ci[l:l + 1, :] * br


def _s5_prep(lam_re, lam_im, log_dt, b_re, b_im):
    lr = lam_re.reshape(DEPTH, S5_NSTATE)
    li = lam_im.reshape(DEPTH, S5_NSTATE)
    ldt = jnp.broadcast_to(log_dt[:, :, None], (DEPTH, S5_GROUPS, S5_STATE)).reshape(DEPTH, S5_NSTATE)
    h_major = lambda b: b.transpose(0, 3, 1, 2).reshape(DEPTH * S5_GROUP, S5_NSTATE)
    vec = jax.ShapeDtypeStruct((DEPTH, S5_NSTATE), _F32)
    mat = jax.ShapeDtypeStruct((DEPTH * S5_GROUP, S5_NSTATE), _F32)
    ar, ai, bbr, bbi = pl.pallas_call(_s5_prep_kernel, out_shape=(vec, vec, mat, mat), name="s5_prep")(
        lr, li, ldt, h_major(b_re), h_major(b_im))
    shape = (DEPTH, S5_GROUP, S5_GROUPS, S5_STATE)
    return ar, ai, bbr.reshape(shape), bbi.reshape(shape)


class _Task:
    def __init__(self, name, unit, cost, deps, fn):
        self.name, self.unit, self.cost, self.deps, self.fn = name, unit, max(1, cost), deps, fn


def _list_schedule(tasks):
    succ = {t.name: [] for t in tasks}
    for t in tasks:
        for d in t.deps:
            succ[d].append(t.name)
    prio = {}
    for t in reversed(tasks):
        prio[t.name] = t.cost + max([prio[s] for s in succ[t.name]], default=0)
    free = {"M": 0, "V": 0}
    finish, order, pending = {}, [], list(tasks)
    while pending:
        best = None
        for t in pending:
            if all(d in finish for d in t.deps):
                start = max([finish[d] for d in t.deps] + [free[t.unit]])
                key = (start, -prio[t.name])
                if best is None or key < best[0]:
                    best = (key, t)
        (start, _), t = best
        finish[t.name] = free[t.unit] = start + t.cost
        order.append(t)
        pending.remove(t)
    return order


def _sigmoid(x):
    return 1.0 / (1.0 + jnp.exp(-x))


def _rms_scale(x):
    return x * lax.rsqrt(jnp.mean(x * x, axis=-1, keepdims=True) + EPS)


def _layer_kernel(nb, tc, n_chunks, pipelined, layer,
                  x_ref, hre0_ref, him0_ref, conv0_ref, ffn0_ref,
                  g_pre_mix_ref, w_in_ref, ar_ref, ai_ref, bbar_ref, c_re_ref, c_im_ref, d_skip_ref,
                  w_glu_ref, b_glu_ref, conv_w_ref, conv_b_ref, ln_g_ref, ln_b_ref, w_out_ref,
                  g_post_mix_ref, g_pre_ffn_ref, w_up_ref, fcw_ref, fcb_ref, w_down_ref, g_post_ffn_ref,
                  y_ref, hre_out_ref, him_out_ref, conv_out_ref, ffn_out_ref,
                  hre_s, him_s, zext_s, tail_s,
                  par1k_s, par512_s, convw_s, fcw_s, assm_s,
                  hba_s, hbb_s, p_s, xs_s, hs_s, y_s, mixin_s, x1_s, x1b_s, e_s, act_s, f_s):
    m = nb * tc
    nbt = nb // SUB
    hist = (CONV_K - 1) * nb
    ftail = (FFN_K - 1) * nb
    multi_chunk = n_chunks > 1
    step = pl.program_id(0)
    last = pl.num_programs(0) - 1
    a_valid = (step < n_chunks) if pipelined else None
    b_valid = (step >= 1) if pipelined else None

    def commit(ref, idx, new, valid):
        ref[idx] = new if valid is None else jnp.where(valid, new, ref[idx])

    @pl.when(step == 0)
    def _prologue():
        pltpu.sync_copy(hre0_ref.at[layer], hre_s)
        pltpu.sync_copy(him0_ref.at[layer], him_s)
        pltpu.sync_copy(conv0_ref.at[layer], zext_s.at[0:hist, :])
        pltpu.sync_copy(ffn0_ref.at[layer], tail_s)

        def rep(src_row):
            return jnp.broadcast_to(src_row, (SUB, src_row.shape[1]))

        def row(ref):
            return ref[layer:layer + 1, :]
        for i, r in enumerate((g_pre_mix_ref, g_post_mix_ref, g_pre_ffn_ref, g_post_ffn_ref)):
            par1k_s[SUB * i:SUB * (i + 1), :] = rep(row(r))
        for i, r in enumerate((conv_b_ref, ln_g_ref, ln_b_ref, d_skip_ref, b_glu_ref)):
            par512_s[SUB * i:SUB * (i + 1), :] = rep(row(r))
        for k in range(CONV_K):
            convw_s[SUB * k:SUB * (k + 1), :] = rep(conv_w_ref[layer, k:k + 1, :])
        for k in range(FFN_K):
            fcw_s[SUB * k:SUB * (k + 1), :] = rep(fcw_ref[layer, k:k + 1, :])
        fcw_s[SUB * FFN_K:SUB * (FFN_K + 1), :] = rep(row(fcb_ref))
        assm_s[0:SUB, :] = rep(row(ar_ref))
        assm_s[SUB:2 * SUB, :] = rep(row(ai_ref))
        if pipelined:
            x1_s[...] = jnp.zeros_like(x1_s)

    def par1k(i):
        return par1k_s[SUB * i:SUB * (i + 1), :]

    def par512(i):
        return par512_s[SUB * i:SUB * (i + 1), :]

    def pair_to_bf16(fn, r):
        return jnp.concatenate([fn(r), fn(r + SUB)], axis=0).astype(_BF16)

    def row_parts(n_parts):
        n_parts = max(1, min(n_parts, m // PACK))
        tiles = m // PACK
        edges = [PACK * ((tiles * i) // n_parts) for i in range(n_parts + 1)]
        return [(edges[i], edges[i + 1]) for i in range(n_parts)]

    tasks = []

    def task(name, unit, cost, deps, fn):
        tasks.append(_Task(name, unit, cost * m // 512, [d for d in deps if d is not None], fn))
        return name

    def staged(prefix, n_parts, cost, deps, fn):
        parts = row_parts(n_parts)
        return [task(f"{prefix}.{i}", "V", cost // len(parts), deps, functools.partial(fn, r0, r1))
                for i, (r0, r1) in enumerate(parts)]

    def ffn_norm(r0, r1):
        g = par1k(2)

        def half(r8):
            x1 = x1_s[r8:r8 + SUB, :]
            x1b_s[r8:r8 + SUB, :] = x1
            return _rms_scale(x1) * g
        for r in range(r0, r1, PACK):
            hbb_s[r:r + PACK, :] = pair_to_bf16(half, r)

    def pre_norm(r0, r1):
        g = par1k(0)
        for r in range(r0, r1, PACK):
            hba_s[r:r + PACK, :] = pair_to_bf16(lambda r8: _rms_scale(x_ref[r8:r8 + SUB, :]) * g, r)

    def in_proj():
        p_s[...] = jnp.dot(hba_s[...], w_in_ref[...], preferred_element_type=_F32)

    def conv_glu(r0, r1):
        for r8 in range(r0, r1, SUB):
            v = p_s[r8:r8 + SUB, S5_WIDTH:S5_WIDTH + CONV_CH]
            gt = p_s[r8:r8 + SUB, S5_WIDTH + CONV_CH:MIX_IN]
            zext_s[hist + r8:hist + r8 + SUB, :] = v * _sigmoid(gt)

    def conv_ln(r0, r1):
        conv_b, ln_g, ln_b = par512(0), par512(1), par512(2)

        def half(r8):
            acc = zext_s[r8:r8 + SUB, :] * convw_s[0:SUB, :]
            for k in range(1, CONV_K):
                acc = acc + zext_s[r8 + k * nb:r8 + k * nb + SUB, :] * convw_s[SUB * k:SUB * (k + 1), :]
            cpre = acc + conv_b
            xc = cpre - jnp.mean(cpre, axis=-1, keepdims=True)
            ln = xc * lax.rsqrt(jnp.mean(xc * xc, axis=-1, keepdims=True) + EPS) * ln_g + ln_b
            return ln * _sigmoid(ln)
        for r in range(r0, r1, PACK):
            mixin_s[r:r + PACK, S5_WIDTH:] = pair_to_bf16(half, r)

    def conv_carry():
        commit(zext_s, (slice(0, hist), slice(None)), zext_s[m:m + hist, :], a_valid)

    def s5_in(j):
        ub = p_s[:, j * S5_BLOCK_IN:(j + 1) * S5_BLOCK_IN].astype(_BF16)
        xs_s[...] = jnp.dot(ub, bbar_ref[j], preferred_element_type=_F32)

    def s5_scan(j):
        cols = slice(j * S5_BLOCK_STATE, (j + 1) * S5_BLOCK_STATE)
        arb = assm_s[0:SUB, cols]
        aib = assm_s[SUB:2 * SUB, cols]
        state = [(hre_s[SUB * bt:SUB * (bt + 1), cols], him_s[SUB * bt:SUB * (bt + 1), cols])
                 for bt in range(nbt)]
        for r in range(0, m, PACK):
            new_r, new_i = [], []
            for r8 in (r, r + SUB):
                bt = (r8 // SUB) % nbt
                hr, hi = state[bt]
                xr = xs_s[r8:r8 + SUB, 0:S5_BLOCK_STATE]
                xi = xs_s[r8:r8 + SUB, S5_BLOCK_STATE:]
                hr_n = arb * hr - aib * hi + xr
                hi_n = arb * hi + aib * hr + xi
                state[bt] = (hr_n, hi_n)
                new_r.append(hr_n)
                new_i.append(hi_n)
            hs_s[j % 2, r:r + PACK, 0:S5_BLOCK_STATE] = jnp.concatenate(new_r, axis=0).astype(_BF16)
            hs_s[j % 2, r:r + PACK, S5_BLOCK_STATE:] = jnp.concatenate(new_i, axis=0).astype(_BF16)
        for bt in range(nbt):
            rows = slice(SUB * bt, SUB * (bt + 1))
            commit(hre_s, (rows, cols), state[bt][0], a_valid)
            commit(him_s, (rows, cols), state[bt][1], a_valid)

    def s5_out(j):
        y_s[:, j * S5_BLOCK_IN:(j + 1) * S5_BLOCK_IN] = (
            jnp.dot(hs_s[j % 2, :, 0:S5_BLOCK_STATE], c_re_ref[j], preferred_element_type=_F32)
            - jnp.dot(hs_s[j % 2, :, S5_BLOCK_STATE:], c_im_ref[j], preferred_element_type=_F32))

    def gelu(r0, r1):
        d_skip = par512(3)
        sqrt_half = math.sqrt(0.5)

        def half(r8):
            yv = y_s[r8:r8 + SUB, :] + d_skip * p_s[r8:r8 + SUB, 0:S5_WIDTH]
            a = 0.5 * yv * (1.0 + lax.erf(yv * sqrt_half))
            y_s[r8:r8 + SUB, :] = a
            return a
        for r in range(r0, r1, PACK):
            mixin_s[r:r + PACK, 0:S5_WIDTH] = pair_to_bf16(half, r)

    def glu_proj():
        p_s[:, 0:S5_WIDTH] = jnp.dot(mixin_s[:, 0:S5_WIDTH], w_glu_ref[...], preferred_element_type=_F32)

    def glu_gate(r0, r1):
        b_glu = par512(4)
        for r in range(r0, r1, PACK):
            mixin_s[r:r + PACK, 0:S5_WIDTH] = pair_to_bf16(
                lambda r8: y_s[r8:r8 + SUB, :] * _sigmoid(p_s[r8:r8 + SUB, 0:S5_WIDTH] + b_glu), r)

    def out_proj():
        p_s[:, 0:D_MODEL] = jnp.dot(mixin_s[...], w_out_ref[...], preferred_element_type=_F32)

    def mix_residual(r0, r1):
        g2 = par1k(1)
        for r8 in range(r0, r1, SUB):
            x1_s[r8:r8 + SUB, :] = x_ref[r8:r8 + SUB, :] + _rms_scale(p_s[r8:r8 + SUB, 0:D_MODEL]) * g2

    def ff_cols(j):
        return slice(j * FF_BLOCK, (j + 1) * FF_BLOCK), slice(D_FF + j * FF_BLOCK, D_FF + (j + 1) * FF_BLOCK)

    def ff_up(j):
        slot = j % FF_SLOTS
        for cols, lanes in zip(ff_cols(j), (slice(0, FF_BLOCK), slice(FF_BLOCK, 2 * FF_BLOCK))):
            e_s[slot, ftail:ftail + m, lanes] = jnp.dot(hbb_s[...], w_up_ref[:, cols], preferred_element_type=_F32)
            e_s[slot, 0:ftail, lanes] = tail_s[:, cols]
            commit(tail_s, (slice(None), cols), e_s[slot, m:m + ftail, lanes], b_valid)

    def ff_act(j, r0, r1):
        slot = j % FF_SLOTS
        gate_cols, value_cols = ff_cols(j)

        def conv3(r8, lanes, cols):
            taps = [fcw_s[SUB * k:SUB * (k + 1), cols] for k in range(FFN_K + 1)]
            return (e_s[slot, r8:r8 + SUB, lanes] * taps[0]
                    + e_s[slot, r8 + nb:r8 + nb + SUB, lanes] * taps[1]
                    + e_s[slot, r8 + 2 * nb:r8 + 2 * nb + SUB, lanes] * taps[2]) + taps[3]

        def half(r8):
            cg = conv3(r8, slice(0, FF_BLOCK), gate_cols)
            cv = conv3(r8, slice(FF_BLOCK, 2 * FF_BLOCK), value_cols)
            return cg * _sigmoid(cg) * cv
        for r in range(r0, r1, PACK):
            act_s[slot, r:r + PACK, :] = pair_to_bf16(half, r)

    def ff_down(j):
        gc = j * FF_BLOCK
        part = jnp.dot(act_s[j % FF_SLOTS], w_down_ref[gc:gc + FF_BLOCK, :], preferred_element_type=_F32)
        if j == 0:
            f_s[...] = part
        else:
            f_s[...] += part

    def ffn_residual(r0, r1):
        g4 = par1k(3)
        for r8 in range(r0, r1, SUB):
            y_ref[r8:r8 + SUB, :] = x1b_s[r8:r8 + SUB, :] + _rms_scale(f_s[r8:r8 + SUB, :]) * g4

    t_pre = staged("pre_norm", 2, 600, [], pre_norm)
    t_in = task("in_proj", "M", 3072, t_pre, in_proj)
    t_glu = staged("conv_glu", 1, 400, [t_in], conv_glu)
    t_conv = staged("conv_ln", 8, 5200, t_glu, conv_ln)
    if multi_chunk:
        t_conv = [task("conv_carry", "V", 60, t_conv, conv_carry)]
    t_scan, t_sout = [], []
    for j in range(S5_BLOCKS):
        t_sin = task(f"s5_in.{j}", "M", 512, [t_in] + t_scan[-1:], functools.partial(s5_in, j))
        t_scan.append(task(f"s5_scan.{j}", "V", 900, [t_sin] + t_sout[-2:-1], functools.partial(s5_scan, j)))
        t_sout.append(task(f"s5_out.{j}", "M", 512, [t_scan[-1]], functools.partial(s5_out, j)))
    t_gelu = staged("gelu", 2, 800, t_sout, gelu)
    t_gproj = task("glu_proj", "M", 512, t_gelu, glu_proj)
    t_gate = staged("glu_gate", 1, 400, [t_gproj], glu_gate)
    t_out = task("out_proj", "M", 2048, t_gate + t_conv, out_proj)
    if pipelined:
        t_fnorm = staged("ffn_norm", 2, 600, [], ffn_norm)
        t_res = staged("mix_residual", 2, 800, [t_out] + t_fnorm, mix_residual)
    else:
        t_res = staged("mix_residual", 2, 800, [t_out], mix_residual)
        t_fnorm = staged("ffn_norm", 2, 600, t_res, ffn_norm)
    t_act, t_down = [], []
    for j in range(FF_NBLOCKS):
        t_up = task(f"ff_up.{j}", "M", 1024, t_fnorm + (t_act[j - FF_SLOTS] if j >= FF_SLOTS else []),
                    functools.partial(ff_up, j))
        t_act.append(staged(f"ff_act.{j}", 2, 800, [t_up] + ([t_down[j - FF_SLOTS]] if j >= FF_SLOTS else []),
                            functools.partial(ff_act, j)))
        t_down.append(task(f"ff_down.{j}", "M", 512, t_act[-1] + t_down[-1:], functools.partial(ff_down, j)))
    staged("ffn_residual", 2, 600, t_down[-1:], ffn_residual)

    for t in _list_schedule(tasks):
        t.fn()

    @pl.when(step == last)
    def _epilogue():
        pltpu.sync_copy(hre_s, hre_out_ref)
        pltpu.sync_copy(him_s, him_out_ref)
        if multi_chunk:
            pltpu.sync_copy(zext_s.at[0:hist, :], conv_out_ref)
        else:
            pltpu.sync_copy(zext_s.at[m:m + hist, :], conv_out_ref)
        pltpu.sync_copy(tail_s, ffn_out_ref)


def _in_hbm():
    return pl.BlockSpec(memory_space=pl.ANY)


def _resident():
    return pl.BlockSpec(memory_space=pltpu.MemorySpace.VMEM)


def _run_layer(nb, tc, layer, x2d, states, w):
    rows = x2d.shape[0]
    m = nb * tc
    assert rows % m == 0 and m % PACK == 0 and nb % SUB == 0
    n_chunks = rows // m
    pipelined = n_chunks > 1
    hist = (CONV_K - 1) * nb
    ftail = (FFN_K - 1) * nb
    assert n_chunks == 1 or m >= hist

    if pipelined:
        n_steps = n_chunks + 1
        x_map = lambda c: (jnp.minimum(c, n_chunks - 1), 0)
        y_map = lambda c: (jnp.maximum(c - 1, 0), 0)
    else:
        n_steps = n_chunks
        x_map = y_map = lambda c: (c, 0)

    in_specs = ([pl.BlockSpec((m, D_MODEL), x_map)]
                + [_in_hbm() for _ in range(4)] + [_resident() for _ in range(22)])
    out_specs = (pl.BlockSpec((m, D_MODEL), y_map),) + tuple(_in_hbm() for _ in range(4))
    out_shape = (
        jax.ShapeDtypeStruct((rows, D_MODEL), _F32),
        jax.ShapeDtypeStruct((nb, S5_NSTATE), _F32), jax.ShapeDtypeStruct((nb, S5_NSTATE), _F32),
        jax.ShapeDtypeStruct((hist, CONV_CH), _F32), jax.ShapeDtypeStruct((ftail, 2 * D_FF), _F32),
    )
    scratch = [
        pltpu.VMEM((nb, S5_NSTATE), _F32),
        pltpu.VMEM((nb, S5_NSTATE), _F32),
        pltpu.VMEM((hist + m, CONV_CH), _F32),
        pltpu.VMEM((ftail, 2 * D_FF), _F32),
        pltpu.VMEM((4 * SUB, D_MODEL), _F32),
        pltpu.VMEM((5 * SUB, CONV_CH), _F32),
        pltpu.VMEM((CONV_K * SUB, CONV_CH), _F32),
        pltpu.VMEM(((FFN_K + 1) * SUB, 2 * D_FF), _F32),
        pltpu.VMEM((2 * SUB, S5_NSTATE), _F32),
        pltpu.VMEM((m, D_MODEL), _BF16),
        pltpu.VMEM((m, D_MODEL), _BF16),
        pltpu.VMEM((m, MIX_IN), _F32),
        pltpu.VMEM((m, 2 * S5_BLOCK_STATE), _F32),
        pltpu.VMEM((2, m, 2 * S5_BLOCK_STATE), _BF16),
        pltpu.VMEM((m, S5_WIDTH), _F32),
        pltpu.VMEM((m, D_MODEL), _BF16),
        pltpu.VMEM((m, D_MODEL), _F32),
        pltpu.VMEM((m, D_MODEL), _F32),
        pltpu.VMEM((FF_SLOTS, ftail + m, 2 * FF_BLOCK), _F32),
        pltpu.VMEM((FF_SLOTS, m, FF_BLOCK), _BF16),
        pltpu.VMEM((m, D_MODEL), _F32),
    ]
    kern = functools.partial(_layer_kernel, nb, tc, n_chunks, pipelined, layer)
    return pl.pallas_call(
        kern,
        grid=(n_steps,),
        in_specs=in_specs,
        out_specs=out_specs,
        out_shape=out_shape,
        scratch_shapes=scratch,
        compiler_params=pltpu.CompilerParams(
            dimension_semantics=("arbitrary",), vmem_limit_bytes=VMEM_LIMIT_BYTES),
        name=f"trunk_layer_nb{nb}",
    )(x2d, *states, *w)


def _block_diag_in(bb):
    t = bb.reshape(S5_GROUP, S5_BLOCKS, 8, S5_STATE).transpose(1, 2, 0, 3)
    eye = jnp.eye(8, dtype=bb.dtype)
    full = t[:, :, :, None, :] * eye[None, :, None, :, None]
    return full.reshape(S5_BLOCKS, 8 * S5_GROUP, 8 * S5_STATE)


def _block_diag_out(c):
    t = c.reshape(S5_BLOCKS, 8, S5_GROUP, S5_STATE).transpose(0, 1, 3, 2)
    eye = jnp.eye(8, dtype=c.dtype)
    full = t[:, :, :, None, :] * eye[None, :, None, :, None]
    return full.reshape(S5_BLOCKS, 8 * S5_STATE, 8 * S5_GROUP)


def kernel(x_prompt, x_sample, state_ssm_re, state_ssm_im, state_conv, state_ffn_conv, g_pre_mix, w_in, lam_re, lam_im, log_dt, b_re, b_im, c_re, c_im, d_skip, w_glu, b_glu, conv_w, conv_b, ln_g, ln_b, w_out, g_post_mix, g_pre_ffn, w_up, ffn_conv_w, ffn_conv_b, w_down, g_post_ffn):
    batch, seq, _ = x_prompt.shape

    ar, ai, bbr, bbi = _s5_prep(lam_re, lam_im, log_dt, b_re, b_im)

    def layer_weights(l):
        bbar = jnp.concatenate([_block_diag_in(bbr[l]), _block_diag_in(bbi[l])], axis=-1).astype(_BF16)
        return (
            g_pre_mix, w_in[l].astype(_BF16), ar, ai, bbar,
            _block_diag_out(c_re[l]).astype(_BF16), _block_diag_out(c_im[l]).astype(_BF16), d_skip,
            w_glu[l].astype(_BF16), b_glu, conv_w, conv_b, ln_g, ln_b,
            w_out[l].astype(_BF16), g_post_mix, g_pre_ffn, w_up[l].astype(_BF16),
            ffn_conv_w, ffn_conv_b, w_down[l].astype(_BF16), g_post_ffn,
        )

    weights = [layer_weights(l) for l in range(DEPTH)]

    def run_trunk(x, ssm_re, ssm_im, conv_buf, ffn_buf, tc):
        nb, length, _ = x.shape
        x2d = x.transpose(1, 0, 2).reshape(length * nb, D_MODEL)
        states = (
            ssm_re.reshape(DEPTH, nb, S5_NSTATE), ssm_im.reshape(DEPTH, nb, S5_NSTATE),
            conv_buf.transpose(0, 2, 1, 3).reshape(DEPTH, (CONV_K - 1) * nb, CONV_CH),
            ffn_buf.transpose(0, 2, 1, 3).reshape(DEPTH, (FFN_K - 1) * nb, 2 * D_FF))
        new = []
        for l in range(DEPTH):
            x2d, *layer_new = _run_layer(nb, tc, l, x2d, states, weights[l])
            new.append(layer_new)
        hre, him, conv_n, ffn_n = (jnp.stack([n[i] for n in new]) for i in range(4))
        y = x2d.reshape(length, nb, D_MODEL).transpose(1, 0, 2)
        return (y, hre.reshape(DEPTH, nb, S5_GROUPS, S5_STATE), him.reshape(DEPTH, nb, S5_GROUPS, S5_STATE),
                conv_n.reshape(DEPTH, CONV_K - 1, nb, CONV_CH).transpose(0, 2, 1, 3),
                ffn_n.reshape(DEPTH, FFN_K - 1, nb, 2 * D_FF).transpose(0, 2, 1, 3))

    dt = x_prompt.dtype
    zeros_ssm = jnp.zeros((DEPTH, batch, S5_GROUPS, S5_STATE), _F32)
    zeros_conv = jnp.zeros((DEPTH, batch, CONV_K - 1, CONV_CH), dt)
    zeros_ffn = jnp.zeros((DEPTH, batch, FFN_K - 1, 2 * D_FF), dt)
    y_p, p_re, p_im, p_conv, p_ffn = run_trunk(x_prompt, zeros_ssm, zeros_ssm, zeros_conv, zeros_ffn, tc=64)
    y_s, s_re, s_im, s_conv, s_ffn = run_trunk(x_sample, state_ssm_re, state_ssm_im, state_conv, state_ffn_conv, tc=1)
    return (y_p, y_s, p_re, p_im, p_conv, p_ffn, s_re, s_im, s_conv, s_ffn)
```

```python
import functools
import math

import jax
import jax.numpy as jnp
from jax import lax
from jax.experimental import pallas as pl
from jax.experimental.pallas import tpu as pltpu

D_MODEL = 1024
DEPTH = 2
S5_WIDTH = 512
S5_GROUP = 16
S5_GROUPS = 32
S5_STATE = 64
S5_NSTATE = S5_GROUPS * S5_STATE
CONV_CH = 512
CONV_K = 31
MIX_IN = S5_WIDTH + 2 * CONV_CH
D_FF = 2816
FFN_K = 3
EPS = 1e-6

SUB = 8
LANES = 128
PACK = 16
S5_BLOCKS = 4
S5_BLOCK_IN = S5_WIDTH // S5_BLOCKS
S5_BLOCK_STATE = S5_NSTATE // S5_BLOCKS
FF_BLOCK = 256
FF_NBLOCKS = D_FF // FF_BLOCK
FF_SLOTS = 2
VMEM_LIMIT_BYTES = 62 * 1024 * 1024

_F32 = jnp.float32
_BF16 = jnp.bfloat16


def _s5_prep_kernel(lr_ref, li_ref, ldt_ref, br_ref, bi_ref, ar_ref, ai_ref, bbr_ref, bbi_ref):
    dt = jnp.exp(ldt_ref[...])
    lr = lr_ref[...]
    li = li_ref[...]
    mag = jnp.exp(lr * dt)
    ar = mag * jnp.cos(li * dt)
    ai = mag * jnp.sin(li * dt)
    den = lr * lr + li * li
    cr = ((ar - 1.0) * lr + ai * li) / den
    ci = (ai * lr - (ar - 1.0) * li) / den
    ar_ref[...] = ar
    ai_ref[...] = ai
    for l in range(DEPTH):
        rows = slice(l * S5_GROUP, (l + 1) * S5_GROUP)
        br = br_ref[rows, :]
        bi = bi_ref[rows, :]
        bbr_ref[rows, :] = cr[l:l + 1, :] * br - ci[l:l + 1, :] * bi
        bbi_ref[rows, :] = cr[l:l + 1, :] * bi + ci[l:l + 1, :] * br


def _s5_prep(lam_re, lam_im, log_dt, b_re, b_im):
    lr = lam_re.reshape(DEPTH, S5_NSTATE)
    li = lam_im.reshape(DEPTH, S5_NSTATE)
    ldt = jnp.broadcast_to(log_dt[:, :, None], (DEPTH, S5_GROUPS, S5_STATE)).reshape(DEPTH, S5_NSTATE)
    h_major = lambda b: b.transpose(0, 3, 1, 2).reshape(DEPTH * S5_GROUP, S5_NSTATE)
    vec = jax.ShapeDtypeStruct((DEPTH, S5_NSTATE), _F32)
    mat = jax.ShapeDtypeStruct((DEPTH * S5_GROUP, S5_NSTATE), _F32)
    ar, ai, bbr, bbi = pl.pallas_call(_s5_prep_kernel, out_shape=(vec, vec, mat, mat), name="s5_prep")(
        lr, li, ldt, h_major(b_re), h_major(b_im))
    shape = (DEPTH, S5_GROUP, S5_GROUPS, S5_STATE)
    return ar, ai, bbr.reshape(shape), bbi.reshape(shape)


class _Task:
    def __init__(self, name, unit, cost, deps, fn):
        self.name, self.unit, self.cost, self.deps, self.fn = name, unit, max(1, cost), deps, fn


def _list_schedule(tasks):
    succ = {t.name: [] for t in tasks}
    for t in tasks:
        for d in t.deps:
            succ[d].append(t.name)
    prio = {}
    for t in reversed(tasks):
        prio[t.name] = t.cost + max([prio[s] for s in succ[t.name]], default=0)
    free = {"M": 0, "V": 0}
    finish, order, pending = {}, [], list(tasks)
    while pending:
        best = None
        for t in pending:
            if all(d in finish for d in t.deps):
                start = max([finish[d] for d in t.deps] + [free[t.unit]])
                key = (start, -prio[t.name])
                if best is None or key < best[0]:
                    best = (key, t)
        (start, _), t = best
        finish[t.name] = free[t.unit] = start + t.cost
        order.append(t)
        pending.remove(t)
    return order


def _sigmoid(x):
    return 1.0 / (1.0 + jnp.exp(-x))


def _wait_for(x, other):
    xb = lax.bitcast_convert_type(x, jnp.int32)
    ob = lax.bitcast_convert_type(other, jnp.int32)
    return lax.bitcast_convert_type(jnp.minimum(xb, jnp.maximum(xb, ob)), x.dtype)


def _rms_scale(x):
    return x * lax.rsqrt(jnp.mean(x * x, axis=-1, keepdims=True) + EPS)


def _layer_kernel(nb, tc, n_chunks, pipelined, layer,
                  x_ref, hre0_ref, him0_ref, conv0_ref, ffn0_ref,
                  g_pre_mix_ref, w_in_ref, ar_ref, ai_ref, bbar_ref, c_re_ref, c_im_ref, d_skip_ref,
                  w_glu_ref, b_glu_ref, conv_w_ref, conv_b_ref, ln_g_ref, ln_b_ref, w_out_ref,
                  g_post_mix_ref, g_pre_ffn_ref, w_up_ref, fcw_ref, fcb_ref, w_down_ref, g_post_ffn_ref,
                  y_ref, hre_out_ref, him_out_ref, conv_out_ref, ffn_out_ref,
                  hre_s, him_s, zext_s, tail_s,
                  par1k_s, par512_s, convw_s, fcw_s, assm_s,
                  hba_s, hbb_s, p_s, cpre_s, xs_s, hs_s, y_s, mixin_s, x1_s, x1b_s, e_s, act_s, f_s):
    m = nb * tc
    nbt = nb // SUB
    hist = (CONV_K - 1) * nb
    ftail = (FFN_K - 1) * nb
    multi_chunk = n_chunks > 1
    step = pl.program_id(0)
    last = pl.num_programs(0) - 1
    a_valid = (step < n_chunks) if pipelined else None
    b_valid = (step >= 1) if pipelined else None

    def commit(ref, idx, new, valid):
        ref[idx] = new if valid is None else jnp.where(valid, new, ref[idx])

    @pl.when(step == 0)
    def _prologue():
        pltpu.sync_copy(hre0_ref.at[layer], hre_s)
        pltpu.sync_copy(him0_ref.at[layer], him_s)
        pltpu.sync_copy(conv0_ref.at[layer], zext_s.at[0:hist, :])
        pltpu.sync_copy(ffn0_ref.at[layer], tail_s)

        def rep(src_row):
            return jnp.broadcast_to(src_row, (SUB, src_row.shape[1]))

        def row(ref):
            return ref[layer:layer + 1, :]
        for i, r in enumerate((g_pre_mix_ref, g_post_mix_ref, g_pre_ffn_ref, g_post_ffn_ref)):
            par1k_s[SUB * i:SUB * (i + 1), :] = rep(row(r))
        for i, r in enumerate((conv_b_ref, ln_g_ref, ln_b_ref, d_skip_ref, b_glu_ref)):
            par512_s[SUB * i:SUB * (i + 1), :] = rep(row(r))
        for k in range(CONV_K):
            convw_s[SUB * k:SUB * (k + 1), :] = rep(conv_w_ref[layer, k:k + 1, :])
        for k in range(FFN_K):
            fcw_s[SUB * k:SUB * (k + 1), :] = rep(fcw_ref[layer, k:k + 1, :])
        fcw_s[SUB * FFN_K:SUB * (FFN_K + 1), :] = rep(row(fcb_ref))
        assm_s[0:SUB, :] = rep(row(ar_ref))
        assm_s[SUB:2 * SUB, :] = rep(row(ai_ref))
        if pipelined:
            x1_s[...] = jnp.zeros_like(x1_s)

    def par1k(i):
        return par1k_s[SUB * i:SUB * (i + 1), :]

    def par512(i):
        return par512_s[SUB * i:SUB * (i + 1), :]

    def pair_to_bf16(fn, r):
        return jnp.concatenate([fn(r), fn(r + SUB)], axis=0).astype(_BF16)

    def row_parts(n_parts):
        n_parts = max(1, min(n_parts, m // PACK))
        tiles = m // PACK
        edges = [PACK * ((tiles * i) // n_parts) for i in range(n_parts + 1)]
        return [(edges[i], edges[i + 1]) for i in range(n_parts)]

    tasks = []

    def task(name, unit, cost, deps, fn):
        tasks.append(_Task(name, unit, cost * m // 512, [d for d in deps if d is not None], fn))
        return name

    def staged(prefix, n_parts, cost, deps, fn):
        parts = row_parts(n_parts)
        return [task(f"{prefix}.{i}", "V", cost // len(parts), deps, functools.partial(fn, r0, r1))
                for i, (r0, r1) in enumerate(parts)]

    def ffn_norm(r0, r1):
        g = par1k(2)

        def half(r8):
            x1 = x1_s[r8:r8 + SUB, :]
            x1b_s[r8:r8 + SUB, :] = x1
            return _rms_scale(x1) * g
        for r in range(r0, r1, PACK):
            hbb_s[r:r + PACK, :] = pair_to_bf16(half, r)

    def pre_norm(r0, r1):
        g = par1k(0)
        for r in range(r0, r1, PACK):
            hba_s[r:r + PACK, :] = pair_to_bf16(lambda r8: _rms_scale(x_ref[r8:r8 + SUB, :]) * g, r)

    def in_proj():
        p_s[...] = jnp.dot(hba_s[...], w_in_ref[...], preferred_element_type=_F32)

    def conv_glu(r0, r1):
        for r8 in range(r0, r1, SUB):
            v = p_s[r8:r8 + SUB, S5_WIDTH:S5_WIDTH + CONV_CH]
            gt = p_s[r8:r8 + SUB, S5_WIDTH + CONV_CH:MIX_IN]
            zext_s[hist + r8:hist + r8 + SUB, :] = v * _sigmoid(gt)

    conv_chain = []
    up_done = {}

    def conv_ln(r0, r1, q):
        conv_b, ln_g, ln_b = par512(0), par512(1), par512(2)
        if q in up_done:
            conv_chain.append(up_done.pop(q))

        def taps_direct(r8):
            acc = zext_s[r8:r8 + SUB, :] * convw_s[0:SUB, :]
            for k in range(1, CONV_K):
                acc = acc + zext_s[r8 + k * nb:r8 + k * nb + SUB, :] * convw_s[SUB * k:SUB * (k + 1), :]
            return acc

        def taps_windowed():
            n = (r1 - r0) // SUB
            for lt in range(CONV_CH // LANES):
                lanes = slice(lt * LANES, (lt + 1) * LANES)
                w = [convw_s[SUB * k:SUB * (k + 1), lanes] for k in range(CONV_K)]
                acc = [None] * n
                for u in range(n + CONV_K - 1):
                    z = zext_s[r0 + SUB * u:r0 + SUB * (u + 1), lanes]
                    if u == 0:
                        while conv_chain:
                            z = _wait_for(z, conv_chain.pop())
                    for i in range(max(0, u - CONV_K + 1), min(n, u + 1)):
                        term = z * w[u - i]
                        acc[i] = term if acc[i] is None else acc[i] + term
                conv_chain.append(acc[-1])
                for i in range(n):
                    cpre_s[r0 + SUB * i:r0 + SUB * (i + 1), lanes] = acc[i]

        windowed = nb == SUB
        if windowed:
            taps_windowed()

        def half(r8):
            cpre = (cpre_s[r8:r8 + SUB, :] if windowed else taps_direct(r8)) + conv_b
            xc = cpre - jnp.mean(cpre, axis=-1, keepdims=True)
            ln = xc * lax.rsqrt(jnp.mean(xc * xc, axis=-1, keepdims=True) + EPS) * ln_g + ln_b
            return ln * _sigmoid(ln)
        for r in range(r0, r1, PACK):
            mixin_s[r:r + PACK, S5_WIDTH:] = pair_to_bf16(half, r)

    def conv_carry():
        commit(zext_s, (slice(0, hist), slice(None)), zext_s[m:m + hist, :], a_valid)

    def s5_in(j):
        ub = p_s[:, j * S5_BLOCK_IN:(j + 1) * S5_BLOCK_IN].astype(_BF16)
        xs_s[...] = jnp.dot(ub, bbar_ref[j], preferred_element_type=_F32)

    def s5_scan(j):
        cols = slice(j * S5_BLOCK_STATE, (j + 1) * S5_BLOCK_STATE)
        arb = assm_s[0:SUB, cols]
        aib = assm_s[SUB:2 * SUB, cols]
        state = [(hre_s[SUB * bt:SUB * (bt + 1), cols], him_s[SUB * bt:SUB * (bt + 1), cols])
                 for bt in range(nbt)]
        for r in range(0, m, PACK):
            new_r, new_i = [], []
            for r8 in (r, r + SUB):
                bt = (r8 // SUB) % nbt
                hr, hi = state[bt]
                xr = xs_s[r8:r8 + SUB, 0:S5_BLOCK_STATE]
                xi = xs_s[r8:r8 + SUB, S5_BLOCK_STATE:]
                hr_n = arb * hr - aib * hi + xr
                hi_n = arb * hi + aib * hr + xi
                state[bt] = (hr_n, hi_n)
                new_r.append(hr_n)
                new_i.append(hi_n)
            hs_s[j % 2, r:r + PACK, 0:S5_BLOCK_STATE] = jnp.concatenate(new_r, axis=0).astype(_BF16)
            hs_s[j % 2, r:r + PACK, S5_BLOCK_STATE:] = jnp.concatenate(new_i, axis=0).astype(_BF16)
        for bt in range(nbt):
            rows = slice(SUB * bt, SUB * (bt + 1))
            commit(hre_s, (rows, cols), state[bt][0], a_valid)
            commit(him_s, (rows, cols), state[bt][1], a_valid)

    def s5_out(j):
        y_s[:, j * S5_BLOCK_IN:(j + 1) * S5_BLOCK_IN] = (
            jnp.dot(hs_s[j % 2, :, 0:S5_BLOCK_STATE], c_re_ref[j], preferred_element_type=_F32)
            - jnp.dot(hs_s[j % 2, :, S5_BLOCK_STATE:], c_im_ref[j], preferred_element_type=_F32))

    def gelu(r0, r1):
        d_skip = par512(3)
        sqrt_half = math.sqrt(0.5)

        def half(r8):
            yv = y_s[r8:r8 + SUB, :] + d_skip * p_s[r8:r8 + SUB, 0:S5_WIDTH]
            a = 0.5 * yv * (1.0 + lax.erf(yv * sqrt_half))
            y_s[r8:r8 + SUB, :] = a
            return a
        for r in range(r0, r1, PACK):
            mixin_s[r:r + PACK, 0:S5_WIDTH] = pair_to_bf16(half, r)

    def glu_proj():
        p_s[:, 0:S5_WIDTH] = jnp.dot(mixin_s[:, 0:S5_WIDTH], w_glu_ref[...], preferred_element_type=_F32)

    def glu_gate(r0, r1):
        b_glu = par512(4)
        for r in range(r0, r1, PACK):
            mixin_s[r:r + PACK, 0:S5_WIDTH] = pair_to_bf16(
                lambda r8: y_s[r8:r8 + SUB, :] * _sigmoid(p_s[r8:r8 + SUB, 0:S5_WIDTH] + b_glu), r)

    def out_proj():
        p_s[:, 0:D_MODEL] = jnp.dot(mixin_s[...], w_out_ref[...], preferred_element_type=_F32)

    def mix_residual(r0, r1):
        g2 = par1k(1)
        for r8 in range(r0, r1, SUB):
            x1_s[r8:r8 + SUB, :] = x_ref[r8:r8 + SUB, :] + _rms_scale(p_s[r8:r8 + SUB, 0:D_MODEL]) * g2

    def ff_cols(j):
        return slice(j * FF_BLOCK, (j + 1) * FF_BLOCK), slice(D_FF + j * FF_BLOCK, D_FF + (j + 1) * FF_BLOCK)

    def ff_up(j):
        slot = j % FF_SLOTS
        for cols, lanes in zip(ff_cols(j), (slice(0, FF_BLOCK), slice(FF_BLOCK, 2 * FF_BLOCK))):
            e_s[slot, ftail:ftail + m, lanes] = jnp.dot(hbb_s[...], w_up_ref[:, cols], preferred_element_type=_F32)
            e_s[slot, 0:ftail, lanes] = tail_s[:, cols]
            commit(tail_s, (slice(None), cols), e_s[slot, m:m + ftail, lanes], b_valid)
        if pipelined and nb == SUB:
            up_done[j] = e_s[slot, ftail + m - SUB:ftail + m, 2 * FF_BLOCK - LANES:]

    def ff_act(j, r0, r1):
        slot = j % FF_SLOTS
        gate_cols, value_cols = ff_cols(j)

        def conv3(lanes, cols):
            taps = [fcw_s[SUB * k:SUB * (k + 1), cols] for k in range(FFN_K + 1)]
            tiles = {}

            def tile(r8):
                if r8 not in tiles:
                    tiles[r8] = e_s[slot, r8:r8 + SUB, lanes]
                return tiles[r8]
            return {r8: (tile(r8) * taps[0] + tile(r8 + nb) * taps[1] + tile(r8 + 2 * nb) * taps[2]) + taps[3]
                    for r8 in range(r0, r1, SUB)}

        cg = conv3(slice(0, FF_BLOCK), gate_cols)
        cv = conv3(slice(FF_BLOCK, 2 * FF_BLOCK), value_cols)
        for r in range(r0, r1, PACK):
            act_s[r:r + PACK, gate_cols] = pair_to_bf16(lambda r8: cg[r8] * _sigmoid(cg[r8]) * cv[r8], r)

    def ff_down():
        f_s[...] = jnp.dot(act_s[...], w_down_ref[...], preferred_element_type=_F32)

    def ffn_residual(r0, r1):
        g4 = par1k(3)
        for r8 in range(r0, r1, SUB):
            y_ref[r8:r8 + SUB, :] = x1b_s[r8:r8 + SUB, :] + _rms_scale(f_s[r8:r8 + SUB, :]) * g4

    t_ups = []

    def ffn_tasks(norm_deps):
        t_fnorm = staged("ffn_norm", 2, 600, norm_deps, ffn_norm)
        t_act = []
        for j in range(FF_NBLOCKS):
            t_ups.append(task(f"ff_up.{j}", "M", 1024, t_fnorm + (t_act[j - FF_SLOTS] if j >= FF_SLOTS else []),
                              functools.partial(ff_up, j)))
            t_act.append(staged(f"ff_act.{j}", 2, 800, t_ups[-1:], functools.partial(ff_act, j)))
        t_down = task("ff_down", "M", 5632, [t for ts in t_act for t in ts], ff_down)
        staged("ffn_residual", 2, 600, [t_down], ffn_residual)
        return t_fnorm

    if pipelined:
        t_fnorm = ffn_tasks([])
    t_pre = staged("pre_norm", 2, 600, [], pre_norm)
    t_in = task("in_proj", "M", 3072, t_pre, in_proj)
    t_glu = staged("conv_glu", 1, 400, [t_in], conv_glu)
    t_conv = [task(f"conv_ln.{q}", "V", 650, t_glu + t_ups[q:q + 1], functools.partial(conv_ln, r0, r1, q))
              for q, (r0, r1) in enumerate(row_parts(8))]
    if multi_chunk:
        t_conv = [task("conv_carry", "V", 60, t_conv, conv_carry)]
    t_scan, t_sout = [], []
    for j in range(S5_BLOCKS):
        t_sin = task(f"s5_in.{j}", "M", 512, [t_in] + t_scan[-1:], functools.partial(s5_in, j))
        t_scan.append(task(f"s5_scan.{j}", "V", 900, [t_sin] + t_sout[-2:-1], functools.partial(s5_scan, j)))
        t_sout.append(task(f"s5_out.{j}", "M", 512, [t_scan[-1]], functools.partial(s5_out, j)))
    t_gelu = staged("gelu", 2, 800, t_sout, gelu)
    t_gproj = task("glu_proj", "M", 512, t_gelu, glu_proj)
    t_gate = staged("glu_gate", 1, 400, [t_gproj], glu_gate)
    t_out = task("out_proj", "M", 2048, t_gate + t_conv, out_proj)
    if pipelined:
        staged("mix_residual", 2, 800, [t_out] + t_fnorm, mix_residual)
    else:
        ffn_tasks(staged("mix_residual", 2, 800, [t_out], mix_residual))

    for t in _list_schedule(tasks):
        t.fn()

    @pl.when(step == last)
    def _epilogue():
        pltpu.sync_copy(hre_s, hre_out_ref)
        pltpu.sync_copy(him_s, him_out_ref)
        if multi_chunk:
            pltpu.sync_copy(zext_s.at[0:hist, :], conv_out_ref)
        else:
            pltpu.sync_copy(zext_s.at[m:m + hist, :], conv_out_ref)
        pltpu.sync_copy(tail_s, ffn_out_ref)


def _in_hbm():
    return pl.BlockSpec(memory_space=pl.ANY)


def _resident():
    return pl.BlockSpec(memory_space=pltpu.MemorySpace.VMEM)


def _run_layer(nb, tc, layer, x2d, states, w):
    rows = x2d.shape[0]
    m = nb * tc
    assert rows % m == 0 and m % PACK == 0 and nb % SUB == 0
    n_chunks = rows // m
    pipelined = n_chunks > 1
    hist = (CONV_K - 1) * nb
    ftail = (FFN_K - 1) * nb
    assert n_chunks == 1 or m >= hist

    if pipelined:
        n_steps = n_chunks + 1
        x_map = lambda c: (jnp.minimum(c, n_chunks - 1), 0)
        y_map = lambda c: (jnp.maximum(c - 1, 0), 0)
    else:
        n_steps = n_chunks
        x_map = y_map = lambda c: (c, 0)

    in_specs = ([pl.BlockSpec((m, D_MODEL), x_map)]
                + [_in_hbm() for _ in range(4)] + [_resident() for _ in range(22)])
    out_specs = (pl.BlockSpec((m, D_MODEL), y_map),) + tuple(_in_hbm() for _ in range(4))
    out_shape = (
        jax.ShapeDtypeStruct((rows, D_MODEL), _F32),
        jax.ShapeDtypeStruct((nb, S5_NSTATE), _F32), jax.ShapeDtypeStruct((nb, S5_NSTATE), _F32),
        jax.ShapeDtypeStruct((hist, CONV_CH), _F32), jax.ShapeDtypeStruct((ftail, 2 * D_FF), _F32),
    )
    scratch = [
        pltpu.VMEM((nb, S5_NSTATE), _F32),
        pltpu.VMEM((nb, S5_NSTATE), _F32),
        pltpu.VMEM((hist + m, CONV_CH), _F32),
        pltpu.VMEM((ftail, 2 * D_FF), _F32),
        pltpu.VMEM((4 * SUB, D_MODEL), _F32),
        pltpu.VMEM((5 * SUB, CONV_CH), _F32),
        pltpu.VMEM((CONV_K * SUB, CONV_CH), _F32),
        pltpu.VMEM(((FFN_K + 1) * SUB, 2 * D_FF), _F32),
        pltpu.VMEM((2 * SUB, S5_NSTATE), _F32),
        pltpu.VMEM((m, D_MODEL), _BF16),
        pltpu.VMEM((m, D_MODEL), _BF16),
        pltpu.VMEM((m, MIX_IN), _F32),
        pltpu.VMEM((m, CONV_CH), _F32),
        pltpu.VMEM((m, 2 * S5_BLOCK_STATE), _F32),
        pltpu.VMEM((2, m, 2 * S5_BLOCK_STATE), _BF16),
        pltpu.VMEM((m, S5_WIDTH), _F32),
        pltpu.VMEM((m, D_MODEL), _BF16),
        pltpu.VMEM((m, D_MODEL), _F32),
        pltpu.VMEM((m, D_MODEL), _F32),
        pltpu.VMEM((FF_SLOTS, ftail + m, 2 * FF_BLOCK), _F32),
        pltpu.VMEM((m, D_FF), _BF16),
        pltpu.VMEM((m, D_MODEL), _F32),
    ]
    kern = functools.partial(_layer_kernel, nb, tc, n_chunks, pipelined, layer)
    return pl.pallas_call(
        kern,
        grid=(n_steps,),
        in_specs=in_specs,
        out_specs=out_specs,
        out_shape=out_shape,
        scratch_shapes=scratch,
        compiler_params=pltpu.CompilerParams(
            dimension_semantics=("arbitrary",), vmem_limit_bytes=VMEM_LIMIT_BYTES),
        name=f"trunk_layer_nb{nb}",
    )(x2d, *states, *w)


def _block_diag_in(bb):
    t = bb.reshape(S5_GROUP, S5_BLOCKS, 8, S5_STATE).transpose(1, 2, 0, 3)
    eye = jnp.eye(8, dtype=bb.dtype)
    full = t[:, :, :, None, :] * eye[None, :, None, :, None]
    return full.reshape(S5_BLOCKS, 8 * S5_GROUP, 8 * S5_STATE)


def _block_diag_out(c):
    t = c.reshape(S5_BLOCKS, 8, S5_GROUP, S5_STATE).transpose(0, 1, 3, 2)
    eye = jnp.eye(8, dtype=c.dtype)
    full = t[:, :, :, None, :] * eye[None, :, None, :, None]
    return full.reshape(S5_BLOCKS, 8 * S5_STATE, 8 * S5_GROUP)


def kernel(x_prompt, x_sample, state_ssm_re, state_ssm_im, state_conv, state_ffn_conv, g_pre_mix, w_in, lam_re, lam_im, log_dt, b_re, b_im, c_re, c_im, d_skip, w_glu, b_glu, conv_w, conv_b, ln_g, ln_b, w_out, g_post_mix, g_pre_ffn, w_up, ffn_conv_w, ffn_conv_b, w_down, g_post_ffn):
    batch, seq, _ = x_prompt.shape

    ar, ai, bbr, bbi = _s5_prep(lam_re, lam_im, log_dt, b_re, b_im)

    def layer_weights(l):
        bbar = jnp.concatenate([_block_diag_in(bbr[l]), _block_diag_in(bbi[l])], axis=-1).astype(_BF16)
        return (
            g_pre_mix, w_in[l].astype(_BF16), ar, ai, bbar,
            _block_diag_out(c_re[l]).astype(_BF16), _block_diag_out(c_im[l]).astype(_BF16), d_skip,
            w_glu[l].astype(_BF16), b_glu, conv_w, conv_b, ln_g, ln_b,
            w_out[l].astype(_BF16), g_post_mix, g_pre_ffn, w_up[l].astype(_BF16),
            ffn_conv_w, ffn_conv_b, w_down[l].astype(_BF16), g_post_ffn,
        )

    weights = [layer_weights(l) for l in range(DEPTH)]

    def run_trunk(x, ssm_re, ssm_im, conv_buf, ffn_buf, tc):
        nb, length, _ = x.shape
        x2d = x.transpose(1, 0, 2).reshape(length * nb, D_MODEL)
        states = (
            ssm_re.reshape(DEPTH, nb, S5_NSTATE), ssm_im.reshape(DEPTH, nb, S5_NSTATE),
            conv_buf.transpose(0, 2, 1, 3).reshape(DEPTH, (CONV_K - 1) * nb, CONV_CH),
            ffn_buf.transpose(0, 2, 1, 3).reshape(DEPTH, (FFN_K - 1) * nb, 2 * D_FF))
        new = []
        for l in range(DEPTH):
            x2d, *layer_new = _run_layer(nb, tc, l, x2d, states, weights[l])
            new.append(layer_new)
        hre, him, conv_n, ffn_n = (jnp.stack([n[i] for n in new]) for i in range(4))
        y = x2d.reshape(length, nb, D_MODEL).transpose(1, 0, 2)
        return (y, hre.reshape(DEPTH, nb, S5_GROUPS, S5_STATE), him.reshape(DEPTH, nb, S5_GROUPS, S5_STATE),
                conv_n.reshape(DEPTH, CONV_K - 1, nb, CONV_CH).transpose(0, 2, 1, 3),
                ffn_n.reshape(DEPTH, FFN_K - 1, nb, 2 * D_FF).transpose(0, 2, 1, 3))

    dt = x_prompt.dtype
    zeros_ssm = jnp.zeros((DEPTH, batch, S5_GROUPS, S5_STATE), _F32)
    zeros_conv = jnp.zeros((DEPTH, batch, CONV_K - 1, CONV_CH), dt)
    zeros_ffn = jnp.zeros((DEPTH, batch, FFN_K - 1, 2 * D_FF), dt)
    y_p, p_re, p_im, p_conv, p_ffn = run_trunk(x_prompt, zeros_ssm, zeros_ssm, zeros_conv, zeros_ffn, tc=64)
    y_s, s_re, s_im, s_conv, s_ffn = run_trunk(x_sample, state_ssm_re, state_ssm_im, state_conv, state_ffn_conv, tc=1)
    return (y_p, y_s, p_re, p_im, p_conv, p_ffn, s_re, s_im, s_conv, s_ffn)
```

```python
import functools
import math

import jax
import jax.numpy as jnp
from jax import lax
from jax.experimental import pallas as pl
from jax.experimental.pallas import tpu as pltpu

D_MODEL = 1024
DEPTH = 2
S5_WIDTH = 512
S5_GROUP = 16
S5_GROUPS = 32
S5_STATE = 64
S5_NSTATE = S5_GROUPS * S5_STATE
CONV_CH = 512
CONV_K = 31
MIX_IN = S5_WIDTH + 2 * CONV_CH
D_FF = 2816
FFN_K = 3
EPS = 1e-6

SUB = 8
LANES = 128
PACK = 16
S5_BLOCKS = 4
S5_BLOCK_IN = S5_WIDTH // S5_BLOCKS
S5_BLOCK_STATE = S5_NSTATE // S5_BLOCKS
FF_BLOCK = 256
FF_NBLOCKS = D_FF // FF_BLOCK
FF_SLOTS = 2
VMEM_LIMIT_BYTES = 62 * 1024 * 1024

_F32 = jnp.float32
_BF16 = jnp.bfloat16


def _s5_prep_kernel(lr_ref, li_ref, ldt_ref, br_ref, bi_ref, ar_ref, ai_ref, bbr_ref, bbi_ref):
    dt = jnp.exp(ldt_ref[...])
    lr = lr_ref[...]
    li = li_ref[...]
    mag = jnp.exp(lr * dt)
    ar = mag * jnp.cos(li * dt)
    ai = mag * jnp.sin(li * dt)
    den = lr * lr + li * li
    cr = ((ar - 1.0) * lr + ai * li) / den
    ci = (ai * lr - (ar - 1.0) * li) / den
    ar_ref[...] = ar
    ai_ref[...] = ai
    for l in range(DEPTH):
        rows = slice(l * S5_GROUP, (l + 1) * S5_GROUP)
        br = br_ref[rows, :]
        bi = bi_ref[rows, :]
        bbr_ref[rows, :] = cr[l:l + 1, :] * br - ci[l:l + 1, :] * bi
        bbi_ref[rows, :] = cr[l:l + 1, :] * bi + ci[l:l + 1, :] * br


def _s5_prep(lam_re, lam_im, log_dt, b_re, b_im):
    lr = lam_re.reshape(DEPTH, S5_NSTATE)
    li = lam_im.reshape(DEPTH, S5_NSTATE)
    ldt = jnp.broadcast_to(log_dt[:, :, None], (DEPTH, S5_GROUPS, S5_STATE)).reshape(DEPTH, S5_NSTATE)
    h_major = lambda b: b.transpose(0, 3, 1, 2).reshape(DEPTH * S5_GROUP, S5_NSTATE)
    vec = jax.ShapeDtypeStruct((DEPTH, S5_NSTATE), _F32)
    mat = jax.ShapeDtypeStruct((DEPTH * S5_GROUP, S5_NSTATE), _F32)
    ar, ai, bbr, bbi = pl.pallas_call(_s5_prep_kernel, out_shape=(vec, vec, mat, mat), name="s5_prep")(
        lr, li, ldt, h_major(b_re), h_major(b_im))
    shape = (DEPTH, S5_GROUP, S5_GROUPS, S5_STATE)
    return ar, ai, bbr.reshape(shape), bbi.reshape(shape)


class _Task:
    def __init__(self, name, unit, cost, deps, fn):
        self.name, self.unit, self.cost, self.deps, self.fn = name, unit, max(1, cost), deps, fn


def _list_schedule(tasks):
    succ = {t.name: [] for t in tasks}
    for t in tasks:
        for d in t.deps:
            succ[d].append(t.name)
    prio = {}
    for t in reversed(tasks):
        prio[t.name] = t.cost + max([prio[s] for s in succ[t.name]], default=0)
    free = {"M": 0, "V": 0}
    finish, order, pending = {}, [], list(tasks)
    while pending:
        best = None
        for t in pending:
            if all(d in finish for d in t.deps):
                start = max([finish[d] for d in t.deps] + [free[t.unit]])
                key = (start, -prio[t.name])
                if best is None or key < best[0]:
                    best = (key, t)
        (start, _), t = best
        finish[t.name] = free[t.unit] = start + t.cost
        order.append(t)
        pending.remove(t)
    return order


def _sigmoid(x):
    return 1.0 / (1.0 + jnp.exp(-x))


def _wait_for(x, other):
    xb = lax.bitcast_convert_type(x, jnp.int32)
    ob = lax.bitcast_convert_type(other, jnp.int32)
    return lax.bitcast_convert_type(jnp.minimum(xb, jnp.maximum(xb, ob)), x.dtype)


def _transpose8(tiles):
    rows = lax.broadcasted_iota(jnp.int32, tiles[0].shape, 0)
    t = list(tiles)
    for d in (4, 2, 1):
        keep = (rows & d) == 0
        for i in range(SUB):
            if i & d == 0:
                a, b = t[i], t[i | d]
                t[i] = jnp.where(keep, a, pltpu.roll(b, d, 0))
                t[i | d] = jnp.where(keep, pltpu.roll(a, SUB - d, 0), b)
    return t


def _rms_scale(x):
    return x * lax.rsqrt(jnp.mean(x * x, axis=-1, keepdims=True) + EPS)


def _layer_kernel(nb, tc, n_chunks, pipelined, layer, x_seq_major, y_seq_major,
                  x_ref, hre0_ref, him0_ref, conv0_ref, ffn0_ref,
                  g_pre_mix_ref, w_in_ref, ar_ref, ai_ref, bbar_ref, c_re_ref, c_im_ref, d_skip_ref,
                  w_glu_ref, b_glu_ref, conv_w_ref, conv_b_ref, ln_g_ref, ln_b_ref, w_out_ref,
                  g_post_mix_ref, g_pre_ffn_ref, w_up_ref, fcw_ref, fcb_ref, w_down_ref, g_post_ffn_ref,
                  y_ref, hre_out_ref, him_out_ref, conv_out_ref, ffn_out_ref,
                  hre_s, him_s, zext_s, tail_s,
                  par1k_s, par512_s, convw_s, fcw_s, assm_s,
                  hba_s, hbb_s, p_s, cpre_s, xs_s, hs_s, y_s, mixin_s, x1_s, x1b_s, e_s, act_s, f_s):
    m = nb * tc
    nbt = nb // SUB
    hist = (CONV_K - 1) * nb
    ftail = (FFN_K - 1) * nb
    multi_chunk = n_chunks > 1
    step = pl.program_id(0)
    last = pl.num_programs(0) - 1
    a_valid = (step < n_chunks) if pipelined else None
    b_valid = (step >= 1) if pipelined else None

    def commit(ref, idx, new, valid):
        ref[idx] = new if valid is None else jnp.where(valid, new, ref[idx])

    @pl.when(step == 0)
    def _prologue():
        pltpu.sync_copy(hre0_ref.at[layer], hre_s)
        pltpu.sync_copy(him0_ref.at[layer], him_s)
        pltpu.sync_copy(conv0_ref.at[layer], zext_s.at[0:hist, :])
        pltpu.sync_copy(ffn0_ref.at[layer], tail_s)

        def rep(src_row):
            return jnp.broadcast_to(src_row, (SUB, src_row.shape[1]))

        def row(ref):
            return ref[layer:layer + 1, :]
        for i, r in enumerate((g_pre_mix_ref, g_post_mix_ref, g_pre_ffn_ref, g_post_ffn_ref)):
            par1k_s[SUB * i:SUB * (i + 1), :] = rep(row(r))
        for i, r in enumerate((conv_b_ref, ln_g_ref, ln_b_ref, d_skip_ref, b_glu_ref)):
            par512_s[SUB * i:SUB * (i + 1), :] = rep(row(r))
        for k in range(CONV_K):
            convw_s[SUB * k:SUB * (k + 1), :] = rep(conv_w_ref[layer, k:k + 1, :])
        for k in range(FFN_K):
            fcw_s[SUB * k:SUB * (k + 1), :] = rep(fcw_ref[layer, k:k + 1, :])
        fcw_s[SUB * FFN_K:SUB * (FFN_K + 1), :] = rep(row(fcb_ref))
        assm_s[0:SUB, :] = rep(row(ar_ref))
        assm_s[SUB:2 * SUB, :] = rep(row(ai_ref))
        if pipelined:
            x1_s[...] = jnp.zeros_like(x1_s)

    def par1k(i):
        return par1k_s[SUB * i:SUB * (i + 1), :]

    def par512(i):
        return par512_s[SUB * i:SUB * (i + 1), :]

    def pair_to_bf16(fn, r):
        return jnp.concatenate([fn(r), fn(r + SUB)], axis=0).astype(_BF16)

    def row_parts(n_parts):
        n_parts = max(1, min(n_parts, m // PACK))
        tiles = m // PACK
        edges = [PACK * ((tiles * i) // n_parts) for i in range(n_parts + 1)]
        return [(edges[i], edges[i + 1]) for i in range(n_parts)]

    tasks = []

    def task(name, unit, cost, deps, fn):
        tasks.append(_Task(name, unit, cost * m // 512, [d for d in deps if d is not None], fn))
        return name

    def staged(prefix, n_parts, cost, deps, fn):
        parts = row_parts(n_parts)
        return [task(f"{prefix}.{i}", "V", cost // len(parts), deps, functools.partial(fn, r0, r1))
                for i, (r0, r1) in enumerate(parts)]

    def ffn_norm(r0, r1):
        g = par1k(2)

        def half(r8):
            x1 = x1_s[r8:r8 + SUB, :]
            x1b_s[r8:r8 + SUB, :] = x1
            return _rms_scale(x1) * g
        for r in range(r0, r1, PACK):
            hbb_s[r:r + PACK, :] = pair_to_bf16(half, r)

    def x_slab(r8, cache):
        if not x_seq_major:
            return x_ref[r8:r8 + SUB, :]
        t0 = (r8 // SUB) // SUB * SUB
        if t0 not in cache:
            cache.clear()
            cache[t0] = _transpose8([x_ref[b, t0:t0 + SUB, :] for b in range(SUB)])
        return cache[t0][r8 // SUB - t0]

    def pre_norm(r0, r1):
        g = par1k(0)
        cache = {}
        for r in range(r0, r1, PACK):
            hba_s[r:r + PACK, :] = pair_to_bf16(lambda r8: _rms_scale(x_slab(r8, cache)) * g, r)

    def in_proj():
        p_s[...] = jnp.dot(hba_s[...], w_in_ref[...], preferred_element_type=_F32)

    def conv_glu(r0, r1):
        for r8 in range(r0, r1, SUB):
            v = p_s[r8:r8 + SUB, S5_WIDTH:S5_WIDTH + CONV_CH]
            gt = p_s[r8:r8 + SUB, S5_WIDTH + CONV_CH:MIX_IN]
            zext_s[hist + r8:hist + r8 + SUB, :] = v * _sigmoid(gt)

    conv_chain = []
    up_done = {}

    def conv_ln(r0, r1, q):
        conv_b, ln_g, ln_b = par512(0), par512(1), par512(2)
        if q in up_done:
            conv_chain.append(up_done.pop(q))

        def taps_direct(r8):
            acc = zext_s[r8:r8 + SUB, :] * convw_s[0:SUB, :]
            for k in range(1, CONV_K):
                acc = acc + zext_s[r8 + k * nb:r8 + k * nb + SUB, :] * convw_s[SUB * k:SUB * (k + 1), :]
            return acc

        def taps_windowed():
            n = (r1 - r0) // SUB
            for lt in range(CONV_CH // LANES):
                lanes = slice(lt * LANES, (lt + 1) * LANES)
                w = [convw_s[SUB * k:SUB * (k + 1), lanes] for k in range(CONV_K)]
                acc = [None] * n
                for u in range(n + CONV_K - 1):
                    z = zext_s[r0 + SUB * u:r0 + SUB * (u + 1), lanes]
                    if u == 0:
                        while conv_chain:
                            z = _wait_for(z, conv_chain.pop())
                    for i in range(max(0, u - CONV_K + 1), min(n, u + 1)):
                        term = z * w[u - i]
                        acc[i] = term if acc[i] is None else acc[i] + term
                conv_chain.append(acc[-1])
                for i in range(n):
                    cpre_s[r0 + SUB * i:r0 + SUB * (i + 1), lanes] = acc[i]

        windowed = nb == SUB
        if windowed:
            taps_windowed()

        def half(r8):
            cpre = (cpre_s[r8:r8 + SUB, :] if windowed else taps_direct(r8)) + conv_b
            xc = cpre - jnp.mean(cpre, axis=-1, keepdims=True)
            ln = xc * lax.rsqrt(jnp.mean(xc * xc, axis=-1, keepdims=True) + EPS) * ln_g + ln_b
            return ln * _sigmoid(ln)
        for r in range(r0, r1, PACK):
            mixin_s[r:r + PACK, S5_WIDTH:] = pair_to_bf16(half, r)

    def conv_carry():
        commit(zext_s, (slice(0, hist), slice(None)), zext_s[m:m + hist, :], a_valid)

    def s5_in(j):
        ub = p_s[:, j * S5_BLOCK_IN:(j + 1) * S5_BLOCK_IN].astype(_BF16)
        xs_s[...] = jnp.dot(ub, bbar_ref[j], preferred_element_type=_F32)

    def s5_scan(j):
        cols = slice(j * S5_BLOCK_STATE, (j + 1) * S5_BLOCK_STATE)
        arb = assm_s[0:SUB, cols]
        aib = assm_s[SUB:2 * SUB, cols]
        state = [(hre_s[SUB * bt:SUB * (bt + 1), cols], him_s[SUB * bt:SUB * (bt + 1), cols])
                 for bt in range(nbt)]
        for r in range(0, m, PACK):
            new_r, new_i = [], []
            for r8 in (r, r + SUB):
                bt = (r8 // SUB) % nbt
                hr, hi = state[bt]
                xr = xs_s[r8:r8 + SUB, 0:S5_BLOCK_STATE]
                xi = xs_s[r8:r8 + SUB, S5_BLOCK_STATE:]
                hr_n = arb * hr - aib * hi + xr
                hi_n = arb * hi + aib * hr + xi
                state[bt] = (hr_n, hi_n)
                new_r.append(hr_n)
                new_i.append(hi_n)
            hs_s[j % 2, r:r + PACK, 0:S5_BLOCK_STATE] = jnp.concatenate(new_r, axis=0).astype(_BF16)
            hs_s[j % 2, r:r + PACK, S5_BLOCK_STATE:] = jnp.concatenate(new_i, axis=0).astype(_BF16)
        for bt in range(nbt):
            rows = slice(SUB * bt, SUB * (bt + 1))
            commit(hre_s, (rows, cols), state[bt][0], a_valid)
            commit(him_s, (rows, cols), state[bt][1], a_valid)

    def s5_out(j):
        y_s[:, j * S5_BLOCK_IN:(j + 1) * S5_BLOCK_IN] = (
            jnp.dot(hs_s[j % 2, :, 0:S5_BLOCK_STATE], c_re_ref[j], preferred_element_type=_F32)
            - jnp.dot(hs_s[j % 2, :, S5_BLOCK_STATE:], c_im_ref[j], preferred_element_type=_F32))

    def gelu(r0, r1):
        d_skip = par512(3)
        sqrt_half = math.sqrt(0.5)

        def half(r8):
            yv = y_s[r8:r8 + SUB, :] + d_skip * p_s[r8:r8 + SUB, 0:S5_WIDTH]
            a = 0.5 * yv * (1.0 + lax.erf(yv * sqrt_half))
            y_s[r8:r8 + SUB, :] = a
            return a
        for r in range(r0, r1, PACK):
            mixin_s[r:r + PACK, 0:S5_WIDTH] = pair_to_bf16(half, r)

    def glu_proj():
        p_s[:, 0:S5_WIDTH] = jnp.dot(mixin_s[:, 0:S5_WIDTH], w_glu_ref[...], preferred_element_type=_F32)

    def glu_gate(r0, r1):
        b_glu = par512(4)
        for r in range(r0, r1, PACK):
            mixin_s[r:r + PACK, 0:S5_WIDTH] = pair_to_bf16(
                lambda r8: y_s[r8:r8 + SUB, :] * _sigmoid(p_s[r8:r8 + SUB, 0:S5_WIDTH] + b_glu), r)

    def out_proj():
        p_s[:, 0:D_MODEL] = jnp.dot(mixin_s[...], w_out_ref[...], preferred_element_type=_F32)

    def mix_residual(r0, r1):
        g2 = par1k(1)
        cache = {}
        for r8 in range(r0, r1, SUB):
            x1_s[r8:r8 + SUB, :] = x_slab(r8, cache) + _rms_scale(p_s[r8:r8 + SUB, 0:D_MODEL]) * g2

    def ff_cols(j):
        return slice(j * FF_BLOCK, (j + 1) * FF_BLOCK), slice(D_FF + j * FF_BLOCK, D_FF + (j + 1) * FF_BLOCK)

    def ff_up(j):
        slot = j % FF_SLOTS
        for cols, lanes in zip(ff_cols(j), (slice(0, FF_BLOCK), slice(FF_BLOCK, 2 * FF_BLOCK))):
            e_s[slot, ftail:ftail + m, lanes] = jnp.dot(hbb_s[...], w_up_ref[:, cols], preferred_element_type=_F32)
            e_s[slot, 0:ftail, lanes] = tail_s[:, cols]
            commit(tail_s, (slice(None), cols), e_s[slot, m:m + ftail, lanes], b_valid)
        if pipelined and nb == SUB:
            up_done[j] = e_s[slot, ftail + m - SUB:ftail + m, 2 * FF_BLOCK - LANES:]

    def ff_act(j, r0, r1):
        slot = j % FF_SLOTS
        gate_cols, value_cols = ff_cols(j)

        def conv3(lanes, cols):
            taps = [fcw_s[SUB * k:SUB * (k + 1), cols] for k in range(FFN_K + 1)]
            tiles = {}

            def tile(r8):
                if r8 not in tiles:
                    tiles[r8] = e_s[slot, r8:r8 + SUB, lanes]
                return tiles[r8]
            return {r8: (tile(r8) * taps[0] + tile(r8 + nb) * taps[1] + tile(r8 + 2 * nb) * taps[2]) + taps[3]
                    for r8 in range(r0, r1, SUB)}

        cg = conv3(slice(0, FF_BLOCK), gate_cols)
        cv = conv3(slice(FF_BLOCK, 2 * FF_BLOCK), value_cols)
        for r in range(r0, r1, PACK):
            act_s[r:r + PACK, gate_cols] = pair_to_bf16(lambda r8: cg[r8] * _sigmoid(cg[r8]) * cv[r8], r)

    def ff_down():
        f_s[...] = jnp.dot(act_s[...], w_down_ref[...], preferred_element_type=_F32)

    def ffn_residual(r0, r1):
        g4 = par1k(3)

        def slab(r8):
            return x1b_s[r8:r8 + SUB, :] + _rms_scale(f_s[r8:r8 + SUB, :]) * g4
        if not y_seq_major:
            for r8 in range(r0, r1, SUB):
                y_ref[r8:r8 + SUB, :] = slab(r8)
            return
        for r64 in range(r0, r1, SUB * SUB):
            t0 = r64 // SUB
            tiles = _transpose8([slab(r64 + SUB * s) for s in range(SUB)])
            for b in range(SUB):
                y_ref[b, t0:t0 + SUB, :] = tiles[b]

    t_ups = []

    def ffn_tasks(norm_deps):
        t_fnorm = staged("ffn_norm", 2, 600, norm_deps, ffn_norm)
        t_act = []
        for j in range(FF_NBLOCKS):
            t_ups.append(task(f"ff_up.{j}", "M", 1024, t_fnorm + (t_act[j - FF_SLOTS] if j >= FF_SLOTS else []),
                              functools.partial(ff_up, j)))
            t_act.append(staged(f"ff_act.{j}", 2, 800, t_ups[-1:], functools.partial(ff_act, j)))
        t_down = task("ff_down", "M", 5632, [t for ts in t_act for t in ts], ff_down)
        staged("ffn_residual", 2, 600, [t_down], ffn_residual)
        return t_fnorm

    if pipelined:
        t_fnorm = ffn_tasks([])
    t_pre = staged("pre_norm", 2, 600, [], pre_norm)
    t_in = task("in_proj", "M", 3072, t_pre, in_proj)
    t_glu = staged("conv_glu", 1, 400, [t_in], conv_glu)
    t_conv = [task(f"conv_ln.{q}", "V", 650, t_glu + t_ups[q:q + 1], functools.partial(conv_ln, r0, r1, q))
              for q, (r0, r1) in enumerate(row_parts(8))]
    if multi_chunk:
        t_conv = [task("conv_carry", "V", 60, t_conv, conv_carry)]
    t_scan, t_sout = [], []
    for j in range(S5_BLOCKS):
        t_sin = task(f"s5_in.{j}", "M", 512, [t_in] + t_scan[-1:], functools.partial(s5_in, j))
        t_scan.append(task(f"s5_scan.{j}", "V", 900, [t_sin] + t_sout[-2:-1], functools.partial(s5_scan, j)))
        t_sout.append(task(f"s5_out.{j}", "M", 512, [t_scan[-1]], functools.partial(s5_out, j)))
    t_gelu = staged("gelu", 2, 800, t_sout, gelu)
    t_gproj = task("glu_proj", "M", 512, t_gelu, glu_proj)
    t_gate = staged("glu_gate", 1, 400, [t_gproj], glu_gate)
    t_out = task("out_proj", "M", 2048, t_gate + t_conv, out_proj)
    if pipelined:
        staged("mix_residual", 2, 800, [t_out] + t_fnorm, mix_residual)
    else:
        ffn_tasks(staged("mix_residual", 2, 800, [t_out], mix_residual))

    for t in _list_schedule(tasks):
        t.fn()

    @pl.when(step == last)
    def _epilogue():
        pltpu.sync_copy(hre_s, hre_out_ref)
        pltpu.sync_copy(him_s, him_out_ref)
        if multi_chunk:
            pltpu.sync_copy(zext_s.at[0:hist, :], conv_out_ref)
        else:
            pltpu.sync_copy(zext_s.at[m:m + hist, :], conv_out_ref)
        pltpu.sync_copy(tail_s, ffn_out_ref)


def _in_hbm():
    return pl.BlockSpec(memory_space=pl.ANY)


def _resident():
    return pl.BlockSpec(memory_space=pltpu.MemorySpace.VMEM)


def _run_layer(nb, tc, layer, x, states, w, x_seq_major=False, y_seq_major=False):
    rows = x.shape[0] * x.shape[1] if x_seq_major else x.shape[0]
    m = nb * tc
    assert nb == SUB or not (x_seq_major or y_seq_major)
    assert rows % m == 0 and m % PACK == 0 and nb % SUB == 0
    n_chunks = rows // m
    pipelined = n_chunks > 1
    hist = (CONV_K - 1) * nb
    ftail = (FFN_K - 1) * nb
    assert n_chunks == 1 or m >= hist

    if pipelined:
        n_steps = n_chunks + 1
        x_chunk = lambda c: jnp.minimum(c, n_chunks - 1)
        y_chunk = lambda c: jnp.maximum(c - 1, 0)
    else:
        n_steps = n_chunks
        x_chunk = y_chunk = lambda c: c

    def io_spec(seq_major, chunk):
        if seq_major:
            return pl.BlockSpec((nb, tc, D_MODEL), lambda c: (0, chunk(c), 0))
        return pl.BlockSpec((m, D_MODEL), lambda c: (chunk(c), 0))

    in_specs = ([io_spec(x_seq_major, x_chunk)]
                + [_in_hbm() for _ in range(4)] + [_resident() for _ in range(22)])
    out_specs = (io_spec(y_seq_major, y_chunk),) + tuple(_in_hbm() for _ in range(4))
    out_shape = (
        jax.ShapeDtypeStruct((nb, rows // nb, D_MODEL) if y_seq_major else (rows, D_MODEL), _F32),
        jax.ShapeDtypeStruct((nb, S5_NSTATE), _F32), jax.ShapeDtypeStruct((nb, S5_NSTATE), _F32),
        jax.ShapeDtypeStruct((hist, CONV_CH), _F32), jax.ShapeDtypeStruct((ftail, 2 * D_FF), _F32),
    )
    scratch = [
        pltpu.VMEM((nb, S5_NSTATE), _F32),
        pltpu.VMEM((nb, S5_NSTATE), _F32),
        pltpu.VMEM((hist + m, CONV_CH), _F32),
        pltpu.VMEM((ftail, 2 * D_FF), _F32),
        pltpu.VMEM((4 * SUB, D_MODEL), _F32),
        pltpu.VMEM((5 * SUB, CONV_CH), _F32),
        pltpu.VMEM((CONV_K * SUB, CONV_CH), _F32),
        pltpu.VMEM(((FFN_K + 1) * SUB, 2 * D_FF), _F32),
        pltpu.VMEM((2 * SUB, S5_NSTATE), _F32),
        pltpu.VMEM((m, D_MODEL), _BF16),
        pltpu.VMEM((m, D_MODEL), _BF16),
        pltpu.VMEM((m, MIX_IN), _F32),
        pltpu.VMEM((m, CONV_CH), _F32),
        pltpu.VMEM((m, 2 * S5_BLOCK_STATE), _F32),
        pltpu.VMEM((2, m, 2 * S5_BLOCK_STATE), _BF16),
        pltpu.VMEM((m, S5_WIDTH), _F32),
        pltpu.VMEM((m, D_MODEL), _BF16),
        pltpu.VMEM((m, D_MODEL), _F32),
        pltpu.VMEM((m, D_MODEL), _F32),
        pltpu.VMEM((FF_SLOTS, ftail + m, 2 * FF_BLOCK), _F32),
        pltpu.VMEM((m, D_FF), _BF16),
        pltpu.VMEM((m, D_MODEL), _F32),
    ]
    kern = functools.partial(_layer_kernel, nb, tc, n_chunks, pipelined, layer, x_seq_major, y_seq_major)
    return pl.pallas_call(
        kern,
        grid=(n_steps,),
        in_specs=in_specs,
        out_specs=out_specs,
        out_shape=out_shape,
        scratch_shapes=scratch,
        compiler_params=pltpu.CompilerParams(
            dimension_semantics=("arbitrary",), vmem_limit_bytes=VMEM_LIMIT_BYTES),
        name=f"trunk_layer_nb{nb}",
    )(x, *states, *w)


def _block_diag_in(bb):
    t = bb.reshape(S5_GROUP, S5_BLOCKS, 8, S5_STATE).transpose(1, 2, 0, 3)
    eye = jnp.eye(8, dtype=bb.dtype)
    full = t[:, :, :, None, :] * eye[None, :, None, :, None]
    return full.reshape(S5_BLOCKS, 8 * S5_GROUP, 8 * S5_STATE)


def _block_diag_out(c):
    t = c.reshape(S5_BLOCKS, 8, S5_GROUP, S5_STATE).transpose(0, 1, 3, 2)
    eye = jnp.eye(8, dtype=c.dtype)
    full = t[:, :, :, None, :] * eye[None, :, None, :, None]
    return full.reshape(S5_BLOCKS, 8 * S5_STATE, 8 * S5_GROUP)


def kernel(x_prompt, x_sample, state_ssm_re, state_ssm_im, state_conv, state_ffn_conv, g_pre_mix, w_in, lam_re, lam_im, log_dt, b_re, b_im, c_re, c_im, d_skip, w_glu, b_glu, conv_w, conv_b, ln_g, ln_b, w_out, g_post_mix, g_pre_ffn, w_up, ffn_conv_w, ffn_conv_b, w_down, g_post_ffn):
    batch, seq, _ = x_prompt.shape

    ar, ai, bbr, bbi = _s5_prep(lam_re, lam_im, log_dt, b_re, b_im)

    def layer_weights(l):
        bbar = jnp.concatenate([_block_diag_in(bbr[l]), _block_diag_in(bbi[l])], axis=-1).astype(_BF16)
        return (
            g_pre_mix, w_in[l].astype(_BF16), ar, ai, bbar,
            _block_diag_out(c_re[l]).astype(_BF16), _block_diag_out(c_im[l]).astype(_BF16), d_skip,
            w_glu[l].astype(_BF16), b_glu, conv_w, conv_b, ln_g, ln_b,
            w_out[l].astype(_BF16), g_post_mix, g_pre_ffn, w_up[l].astype(_BF16),
            ffn_conv_w, ffn_conv_b, w_down[l].astype(_BF16), g_post_ffn,
        )

    weights = [layer_weights(l) for l in range(DEPTH)]

    def run_trunk(x, ssm_re, ssm_im, conv_buf, ffn_buf, tc):
        nb, length, _ = x.shape
        in_kernel = nb == SUB
        act = x if in_kernel else x.transpose(1, 0, 2).reshape(length * nb, D_MODEL)
        states = (
            ssm_re.reshape(DEPTH, nb, S5_NSTATE), ssm_im.reshape(DEPTH, nb, S5_NSTATE),
            conv_buf.transpose(0, 2, 1, 3).reshape(DEPTH, (CONV_K - 1) * nb, CONV_CH),
            ffn_buf.transpose(0, 2, 1, 3).reshape(DEPTH, (FFN_K - 1) * nb, 2 * D_FF))
        new = []
        for l in range(DEPTH):
            act, *layer_new = _run_layer(nb, tc, l, act, states, weights[l],
                                         x_seq_major=in_kernel and l == 0,
                                         y_seq_major=in_kernel and l == DEPTH - 1)
            new.append(layer_new)
        hre, him, conv_n, ffn_n = (jnp.stack([n[i] for n in new]) for i in range(4))
        y = act if in_kernel else act.reshape(length, nb, D_MODEL).transpose(1, 0, 2)
        return (y, hre.reshape(DEPTH, nb, S5_GROUPS, S5_STATE), him.reshape(DEPTH, nb, S5_GROUPS, S5_STATE),
                conv_n.reshape(DEPTH, CONV_K - 1, nb, CONV_CH).transpose(0, 2, 1, 3),
                ffn_n.reshape(DEPTH, FFN_K - 1, nb, 2 * D_FF).transpose(0, 2, 1, 3))

    dt = x_prompt.dtype
    zeros_ssm = jnp.zeros((DEPTH, batch, S5_GROUPS, S5_STATE), _F32)
    zeros_conv = jnp.zeros((DEPTH, batch, CONV_K - 1, CONV_CH), dt)
    zeros_ffn = jnp.zeros((DEPTH, batch, FFN_K - 1, 2 * D_FF), dt)
    y_p, p_re, p_im, p_conv, p_ffn = run_trunk(x_prompt, zeros_ssm, zeros_ssm, zeros_conv, zeros_ffn, tc=64)
    y_s, s_re, s_im, s_conv, s_ffn = run_trunk(x_sample, state_ssm_re, state_ssm_im, state_conv, state_ffn_conv, tc=1)
    return (y_p, y_s, p_re, p_im, p_conv, p_ffn, s_re, s_im, s_conv, s_ffn)
```

```python
import functools
import math

import jax
import jax.numpy as jnp
from jax import lax
from jax.experimental import pallas as pl
from jax.experimental.pallas import tpu as pltpu

D_MODEL = 1024
DEPTH = 2
S5_WIDTH = 512
S5_GROUP = 16
S5_GROUPS = 32
S5_STATE = 64
S5_NSTATE = S5_GROUPS * S5_STATE
CONV_CH = 512
CONV_K = 31
MIX_IN = S5_WIDTH + 2 * CONV_CH
D_FF = 2816
FFN_K = 3
EPS = 1e-6

SUB = 8
LANES = 128
PACK = 16
S5_BLOCKS = 4
S5_BLOCK_IN = S5_WIDTH // S5_BLOCKS
S5_BLOCK_STATE = S5_NSTATE // S5_BLOCKS
FF_BLOCK = 256
FF_NBLOCKS = D_FF // FF_BLOCK
FF_SLOTS = 6
VMEM_LIMIT_BYTES = 62 * 1024 * 1024

_F32 = jnp.float32
_BF16 = jnp.bfloat16


def _s5_prep_kernel(lr_ref, li_ref, ldt_ref, br_ref, bi_ref, ar_ref, ai_ref, bbr_ref, bbi_ref):
    dt = jnp.exp(ldt_ref[...])
    lr = lr_ref[...]
    li = li_ref[...]
    mag = jnp.exp(lr * dt)
    ar = mag * jnp.cos(li * dt)
    ai = mag * jnp.sin(li * dt)
    den = lr * lr + li * li
    cr = ((ar - 1.0) * lr + ai * li) / den
    ci = (ai * lr - (ar - 1.0) * li) / den
    ar_ref[...] = ar
    ai_ref[...] = ai
    for l in range(DEPTH):
        rows = slice(l * S5_GROUP, (l + 1) * S5_GROUP)
        br = br_ref[rows, :]
        bi = bi_ref[rows, :]
        bbr_ref[rows, :] = cr[l:l + 1, :] * br - ci[l:l + 1, :] * bi
        bbi_ref[rows, :] = cr[l:l + 1, :] * bi + ci[l:l + 1, :] * br


def _s5_prep(lam_re, lam_im, log_dt, b_re, b_im):
    lr = lam_re.reshape(DEPTH, S5_NSTATE)
    li = lam_im.reshape(DEPTH, S5_NSTATE)
    ldt = jnp.broadcast_to(log_dt[:, :, None], (DEPTH, S5_GROUPS, S5_STATE)).reshape(DEPTH, S5_NSTATE)
    h_major = lambda b: b.transpose(0, 3, 1, 2).reshape(DEPTH * S5_GROUP, S5_NSTATE)
    vec = jax.ShapeDtypeStruct((DEPTH, S5_NSTATE), _F32)
    mat = jax.ShapeDtypeStruct((DEPTH * S5_GROUP, S5_NSTATE), _F32)
    ar, ai, bbr, bbi = pl.pallas_call(_s5_prep_kernel, out_shape=(vec, vec, mat, mat), name="s5_prep")(
        lr, li, ldt, h_major(b_re), h_major(b_im))
    shape = (DEPTH, S5_GROUP, S5_GROUPS, S5_STATE)
    return ar, ai, bbr.reshape(shape), bbi.reshape(shape)


class _Task:
    def __init__(self, name, unit, cost, deps, fn):
        self.name, self.unit, self.cost, self.deps, self.fn = name, unit, max(1, cost), deps, fn


def _list_schedule(tasks):
    succ = {t.name: [] for t in tasks}
    for t in tasks:
        for d in t.deps:
            succ[d].append(t.name)
    prio = {}
    for t in reversed(tasks):
        prio[t.name] = t.cost + max([prio[s] for s in succ[t.name]], default=0)
    free = {"M": 0, "V": 0}
    finish, order, pending = {}, [], list(tasks)
    while pending:
        best = None
        for t in pending:
            if all(d in finish for d in t.deps):
                start = max([finish[d] for d in t.deps] + [free[t.unit]])
                key = (start, -prio[t.name])
                if best is None or key < best[0]:
                    best = (key, t)
        (start, _), t = best
        finish[t.name] = free[t.unit] = start + t.cost
        order.append(t)
        pending.remove(t)
    return order


def _sigmoid(x):
    return 1.0 / (1.0 + jnp.exp(-x))


def _wait_for(x, other):
    xb = lax.bitcast_convert_type(x, jnp.int32)
    ob = lax.bitcast_convert_type(other, jnp.int32)
    return lax.bitcast_convert_type(jnp.minimum(xb, jnp.maximum(xb, ob)), x.dtype)


def _transpose8(tiles):
    rows = lax.broadcasted_iota(jnp.int32, tiles[0].shape, 0)
    t = list(tiles)
    for d in (4, 2, 1):
        keep = (rows & d) == 0
        for i in range(SUB):
            if i & d == 0:
                a, b = t[i], t[i | d]
                t[i] = jnp.where(keep, a, pltpu.roll(b, d, 0))
                t[i | d] = jnp.where(keep, pltpu.roll(a, SUB - d, 0), b)
    return t


def _rms_scale(x):
    return x * lax.rsqrt(jnp.mean(x * x, axis=-1, keepdims=True) + EPS)


def _layer_kernel(nb, tc, n_chunks, pipelined, layer, x_seq_major, y_seq_major,
                  x_ref, hre0_ref, him0_ref, conv0_ref, ffn0_ref,
                  g_pre_mix_ref, w_in_ref, ar_ref, ai_ref, bbar_ref, c_re_ref, c_im_ref, d_skip_ref,
                  w_glu_ref, b_glu_ref, conv_w_ref, conv_b_ref, ln_g_ref, ln_b_ref, w_out_ref,
                  g_post_mix_ref, g_pre_ffn_ref, w_up_ref, fcw_ref, fcb_ref, w_down_ref, g_post_ffn_ref,
                  y_ref, hre_out_ref, him_out_ref, conv_out_ref, ffn_out_ref,
                  hre_s, him_s, zext_s, tail_s,
                  par1k_s, par512_s, convw_s, fcw_s, assm_s,
                  xt_s, hba_s, hbb_s, p_s, cpre_s, xs_s, hs_s, y_s, mixin_s, x1_s, x1b_s, e_s, act_s, f_s):
    m = nb * tc
    nbt = nb // SUB
    hist = (CONV_K - 1) * nb
    ftail = (FFN_K - 1) * nb
    multi_chunk = n_chunks > 1
    step = pl.program_id(0)
    last = pl.num_programs(0) - 1
    a_valid = (step < n_chunks) if pipelined else None
    b_valid = (step >= 1) if pipelined else None

    def commit(ref, idx, new, valid):
        ref[idx] = new if valid is None else jnp.where(valid, new, ref[idx])

    @pl.when(step == 0)
    def _prologue():
        pltpu.sync_copy(hre0_ref.at[layer], hre_s)
        pltpu.sync_copy(him0_ref.at[layer], him_s)
        pltpu.sync_copy(conv0_ref.at[layer], zext_s.at[0:hist, :])
        pltpu.sync_copy(ffn0_ref.at[layer], tail_s)

        def rep(src_row):
            return jnp.broadcast_to(src_row, (SUB, src_row.shape[1]))

        def row(ref):
            return ref[layer:layer + 1, :]
        for i, r in enumerate((g_pre_mix_ref, g_post_mix_ref, g_pre_ffn_ref, g_post_ffn_ref)):
            par1k_s[SUB * i:SUB * (i + 1), :] = rep(row(r))
        for i, r in enumerate((conv_b_ref, ln_g_ref, ln_b_ref, d_skip_ref, b_glu_ref)):
            par512_s[SUB * i:SUB * (i + 1), :] = rep(row(r))
        for k in range(CONV_K):
            convw_s[SUB * k:SUB * (k + 1), :] = rep(conv_w_ref[layer, k:k + 1, :])
        for k in range(FFN_K):
            fcw_s[SUB * k:SUB * (k + 1), :] = rep(fcw_ref[layer, k:k + 1, :])
        fcw_s[SUB * FFN_K:SUB * (FFN_K + 1), :] = rep(row(fcb_ref))
        assm_s[0:SUB, :] = rep(row(ar_ref))
        assm_s[SUB:2 * SUB, :] = rep(row(ai_ref))
        if pipelined:
            x1_s[...] = jnp.zeros_like(x1_s)

    def par1k(i):
        return par1k_s[SUB * i:SUB * (i + 1), :]

    def par512(i):
        return par512_s[SUB * i:SUB * (i + 1), :]

    def pair_to_bf16(fn, r):
        return jnp.concatenate([fn(r), fn(r + SUB)], axis=0).astype(_BF16)

    def row_parts(n_parts):
        n_parts = max(1, min(n_parts, m // PACK))
        tiles = m // PACK
        edges = [PACK * ((tiles * i) // n_parts) for i in range(n_parts + 1)]
        return [(edges[i], edges[i + 1]) for i in range(n_parts)]

    tasks = []

    def task(name, unit, cost, deps, fn):
        tasks.append(_Task(name, unit, cost * m // 512, [d for d in deps if d is not None], fn))
        return name

    def staged(prefix, n_parts, cost, deps, fn):
        parts = row_parts(n_parts)
        return [task(f"{prefix}.{i}", "V", cost // len(parts), deps, functools.partial(fn, r0, r1))
                for i, (r0, r1) in enumerate(parts)]

    def ffn_norm(r0, r1):
        g = par1k(2)

        def half(r8):
            x1 = x1_s[r8:r8 + SUB, :]
            x1b_s[r8:r8 + SUB, :] = x1
            return _rms_scale(x1) * g
        for r in range(r0, r1, PACK):
            hbb_s[r:r + PACK, :] = pair_to_bf16(half, r)

    def pre_norm(r0, r1):
        g = par1k(0)
        cache = {}

        def x_slab(r8):
            if not x_seq_major:
                return x_ref[r8:r8 + SUB, :]
            t0 = (r8 // SUB) // SUB * SUB
            if t0 not in cache:
                cache.clear()
                cache[t0] = _transpose8([x_ref[b, t0:t0 + SUB, :] for b in range(SUB)])
            xt_s[r8:r8 + SUB, :] = cache[t0][r8 // SUB - t0]
            return cache[t0][r8 // SUB - t0]
        for r in range(r0, r1, PACK):
            hba_s[r:r + PACK, :] = pair_to_bf16(lambda r8: _rms_scale(x_slab(r8)) * g, r)

    def in_proj():
        p_s[...] = jnp.dot(hba_s[...], w_in_ref[...], preferred_element_type=_F32)

    def conv_glu(r0, r1):
        for r8 in range(r0, r1, SUB):
            v = p_s[r8:r8 + SUB, S5_WIDTH:S5_WIDTH + CONV_CH]
            gt = p_s[r8:r8 + SUB, S5_WIDTH + CONV_CH:MIX_IN]
            zext_s[hist + r8:hist + r8 + SUB, :] = v * _sigmoid(gt)

    conv_chain = []
    up_done = {}

    def conv_ln(r0, r1, q):
        conv_b, ln_g, ln_b = par512(0), par512(1), par512(2)
        if q in up_done:
            conv_chain.append(up_done.pop(q))

        def taps_direct(r8):
            acc = zext_s[r8:r8 + SUB, :] * convw_s[0:SUB, :]
            for k in range(1, CONV_K):
                acc = acc + zext_s[r8 + k * nb:r8 + k * nb + SUB, :] * convw_s[SUB * k:SUB * (k + 1), :]
            return acc

        def taps_windowed():
            n = (r1 - r0) // SUB
            for lt in range(CONV_CH // LANES):
                lanes = slice(lt * LANES, (lt + 1) * LANES)
                w = [convw_s[SUB * k:SUB * (k + 1), lanes] for k in range(CONV_K)]
                acc = [None] * n
                for u in range(n + CONV_K - 1):
                    z = zext_s[r0 + SUB * u:r0 + SUB * (u + 1), lanes]
                    if u == 0:
                        while conv_chain:
                            z = _wait_for(z, conv_chain.pop())
                    for i in range(max(0, u - CONV_K + 1), min(n, u + 1)):
                        term = z * w[u - i]
                        acc[i] = term if acc[i] is None else acc[i] + term
                conv_chain.append(acc[-1])
                for i in range(n):
                    cpre_s[r0 + SUB * i:r0 + SUB * (i + 1), lanes] = acc[i]

        windowed = nb == SUB
        if windowed:
            taps_windowed()

        def half(r8):
            cpre = (cpre_s[r8:r8 + SUB, :] if windowed else taps_direct(r8)) + conv_b
            xc = cpre - jnp.mean(cpre, axis=-1, keepdims=True)
            ln = xc * lax.rsqrt(jnp.mean(xc * xc, axis=-1, keepdims=True) + EPS) * ln_g + ln_b
            return ln * _sigmoid(ln)
        for r in range(r0, r1, PACK):
            mixin_s[r:r + PACK, S5_WIDTH:] = pair_to_bf16(half, r)

    def conv_carry():
        commit(zext_s, (slice(0, hist), slice(None)), zext_s[m:m + hist, :], a_valid)

    def s5_in(j):
        ub = p_s[:, j * S5_BLOCK_IN:(j + 1) * S5_BLOCK_IN].astype(_BF16)
        xs_s[j % 2] = jnp.dot(ub, bbar_ref[j], preferred_element_type=_F32)

    def s5_scan(j):
        cols = slice(j * S5_BLOCK_STATE, (j + 1) * S5_BLOCK_STATE)
        arb = assm_s[0:SUB, cols]
        aib = assm_s[SUB:2 * SUB, cols]
        state = [(hre_s[SUB * bt:SUB * (bt + 1), cols], him_s[SUB * bt:SUB * (bt + 1), cols])
                 for bt in range(nbt)]
        for r in range(0, m, PACK):
            new_r, new_i = [], []
            for r8 in (r, r + SUB):
                bt = (r8 // SUB) % nbt
                hr, hi = state[bt]
                xr = xs_s[j % 2, r8:r8 + SUB, 0:S5_BLOCK_STATE]
                xi = xs_s[j % 2, r8:r8 + SUB, S5_BLOCK_STATE:]
                hr_n = arb * hr - aib * hi + xr
                hi_n = arb * hi + aib * hr + xi
                state[bt] = (hr_n, hi_n)
                new_r.append(hr_n)
                new_i.append(hi_n)
            hs_s[j % 2, r:r + PACK, 0:S5_BLOCK_STATE] = jnp.concatenate(new_r, axis=0).astype(_BF16)
            hs_s[j % 2, r:r + PACK, S5_BLOCK_STATE:] = jnp.concatenate(new_i, axis=0).astype(_BF16)
        for bt in range(nbt):
            rows = slice(SUB * bt, SUB * (bt + 1))
            commit(hre_s, (rows, cols), state[bt][0], a_valid)
            commit(him_s, (rows, cols), state[bt][1], a_valid)

    def s5_out(j):
        y_s[:, j * S5_BLOCK_IN:(j + 1) * S5_BLOCK_IN] = (
            jnp.dot(hs_s[j % 2, :, 0:S5_BLOCK_STATE], c_re_ref[j], preferred_element_type=_F32)
            - jnp.dot(hs_s[j % 2, :, S5_BLOCK_STATE:], c_im_ref[j], preferred_element_type=_F32))

    def gelu(r0, r1):
        d_skip = par512(3)
        sqrt_half = math.sqrt(0.5)

        def half(r8):
            yv = y_s[r8:r8 + SUB, :] + d_skip * p_s[r8:r8 + SUB, 0:S5_WIDTH]
            a = 0.5 * yv * (1.0 + lax.erf(yv * sqrt_half))
            y_s[r8:r8 + SUB, :] = a
            return a
        for r in range(r0, r1, PACK):
            mixin_s[r:r + PACK, 0:S5_WIDTH] = pair_to_bf16(half, r)

    def glu_proj():
        p_s[:, 0:S5_WIDTH] = jnp.dot(mixin_s[:, 0:S5_WIDTH], w_glu_ref[...], preferred_element_type=_F32)

    def glu_gate(r0, r1):
        b_glu = par512(4)
        for r in range(r0, r1, PACK):
            mixin_s[r:r + PACK, 0:S5_WIDTH] = pair_to_bf16(
                lambda r8: y_s[r8:r8 + SUB, :] * _sigmoid(p_s[r8:r8 + SUB, 0:S5_WIDTH] + b_glu), r)

    def out_proj():
        p_s[:, 0:D_MODEL] = jnp.dot(mixin_s[...], w_out_ref[...], preferred_element_type=_F32)

    def mix_residual(r0, r1):
        g2 = par1k(1)
        x_rows = xt_s if x_seq_major else x_ref
        for r8 in range(r0, r1, SUB):
            x1_s[r8:r8 + SUB, :] = x_rows[r8:r8 + SUB, :] + _rms_scale(p_s[r8:r8 + SUB, 0:D_MODEL]) * g2

    def ff_cols(j):
        return slice(j * FF_BLOCK, (j + 1) * FF_BLOCK), slice(D_FF + j * FF_BLOCK, D_FF + (j + 1) * FF_BLOCK)

    def ff_up(j):
        slot = j % FF_SLOTS
        for cols, lanes in zip(ff_cols(j), (slice(0, FF_BLOCK), slice(FF_BLOCK, 2 * FF_BLOCK))):
            e_s[slot, ftail:ftail + m, lanes] = jnp.dot(hbb_s[...], w_up_ref[:, cols], preferred_element_type=_F32)
            e_s[slot, 0:ftail, lanes] = tail_s[:, cols]
            commit(tail_s, (slice(None), cols), e_s[slot, m:m + ftail, lanes], b_valid)
        if pipelined and nb == SUB:
            up_done[j] = e_s[slot, ftail + m - SUB:ftail + m, 2 * FF_BLOCK - LANES:]

    def ff_act(j, r0, r1):
        slot = j % FF_SLOTS
        gate_cols, value_cols = ff_cols(j)

        def conv3(lanes, cols):
            taps = [fcw_s[SUB * k:SUB * (k + 1), cols] for k in range(FFN_K + 1)]
            tiles = {}

            def tile(r8):
                if r8 not in tiles:
                    tiles[r8] = e_s[slot, r8:r8 + SUB, lanes]
                return tiles[r8]
            return {r8: (tile(r8) * taps[0] + tile(r8 + nb) * taps[1] + tile(r8 + 2 * nb) * taps[2]) + taps[3]
                    for r8 in range(r0, r1, SUB)}

        cg = conv3(slice(0, FF_BLOCK), gate_cols)
        cv = conv3(slice(FF_BLOCK, 2 * FF_BLOCK), value_cols)
        for r in range(r0, r1, PACK):
            act_s[r:r + PACK, gate_cols] = pair_to_bf16(lambda r8: cg[r8] * _sigmoid(cg[r8]) * cv[r8], r)

    def ff_down():
        f_s[...] = jnp.dot(act_s[...], w_down_ref[...], preferred_element_type=_F32)

    def ffn_residual(r0, r1):
        g4 = par1k(3)

        def slab(r8):
            return x1b_s[r8:r8 + SUB, :] + _rms_scale(f_s[r8:r8 + SUB, :]) * g4
        if not y_seq_major:
            for r8 in range(r0, r1, SUB):
                y_ref[r8:r8 + SUB, :] = slab(r8)
            return
        for r64 in range(r0, r1, SUB * SUB):
            t0 = r64 // SUB
            tiles = _transpose8([slab(r64 + SUB * s) for s in range(SUB)])
            for b in range(SUB):
                y_ref[b, t0:t0 + SUB, :] = tiles[b]

    t_ups = []

    def ffn_tasks(norm_deps):
        t_fnorm = staged("ffn_norm", 2, 600, norm_deps, ffn_norm)
        t_act = []
        for j in range(FF_NBLOCKS):
            t_ups.append(task(f"ff_up.{j}", "M", 1024, t_fnorm + (t_act[j - FF_SLOTS] if j >= FF_SLOTS else []),
                              functools.partial(ff_up, j)))
            t_act.append(staged(f"ff_act.{j}", 2, 800, t_ups[-1:], functools.partial(ff_act, j)))
        t_down = task("ff_down", "M", 5632, [t for ts in t_act for t in ts], ff_down)
        staged("ffn_residual", 2, 600, [t_down], ffn_residual)
        return t_fnorm

    if pipelined:
        t_fnorm = ffn_tasks([])
    t_pre = staged("pre_norm", 2, 600, [], pre_norm)
    t_in = task("in_proj", "M", 3072, t_pre, in_proj)
    t_glu = staged("conv_glu", 1, 400, [t_in], conv_glu)
    t_conv = [task(f"conv_ln.{q}", "V", 650, t_glu + t_ups[q:q + 1], functools.partial(conv_ln, r0, r1, q))
              for q, (r0, r1) in enumerate(row_parts(8))]
    if multi_chunk:
        t_conv = [task("conv_carry", "V", 60, t_conv, conv_carry)]
    t_scan, t_sout = [], []
    for j in range(S5_BLOCKS):
        t_sin = task(f"s5_in.{j}", "M", 512, [t_in] + t_scan[-2:-1], functools.partial(s5_in, j))
        t_scan.append(task(f"s5_scan.{j}", "V", 900, [t_sin] + t_sout[-2:-1], functools.partial(s5_scan, j)))
        t_sout.append(task(f"s5_out.{j}", "M", 512, [t_scan[-1]], functools.partial(s5_out, j)))
    t_gelu = staged("gelu", 2, 800, t_sout, gelu)
    t_gproj = task("glu_proj", "M", 512, t_gelu, glu_proj)
    t_gate = staged("glu_gate", 1, 400, [t_gproj], glu_gate)
    t_out = task("out_proj", "M", 2048, t_gate + t_conv, out_proj)
    if pipelined:
        staged("mix_residual", 2, 800, [t_out] + t_fnorm, mix_residual)
    else:
        ffn_tasks(staged("mix_residual", 2, 800, [t_out], mix_residual))

    for t in _list_schedule(tasks):
        t.fn()

    @pl.when(step == last)
    def _epilogue():
        pltpu.sync_copy(hre_s, hre_out_ref)
        pltpu.sync_copy(him_s, him_out_ref)
        if multi_chunk:
            pltpu.sync_copy(zext_s.at[0:hist, :], conv_out_ref)
        else:
            pltpu.sync_copy(zext_s.at[m:m + hist, :], conv_out_ref)
        pltpu.sync_copy(tail_s, ffn_out_ref)


def _in_hbm():
    return pl.BlockSpec(memory_space=pl.ANY)


def _resident():
    return pl.BlockSpec(memory_space=pltpu.MemorySpace.VMEM)


def _run_layer(nb, tc, layer, x, states, w, x_seq_major=False, y_seq_major=False):
    rows = x.shape[0] * x.shape[1] if x_seq_major else x.shape[0]
    m = nb * tc
    assert nb == SUB or not (x_seq_major or y_seq_major)
    assert rows % m == 0 and m % PACK == 0 and nb % SUB == 0
    n_chunks = rows // m
    pipelined = n_chunks > 1
    hist = (CONV_K - 1) * nb
    ftail = (FFN_K - 1) * nb
    assert n_chunks == 1 or m >= hist

    if pipelined:
        n_steps = n_chunks + 1
        x_chunk = lambda c: jnp.minimum(c, n_chunks - 1)
        y_chunk = lambda c: jnp.maximum(c - 1, 0)
    else:
        n_steps = n_chunks
        x_chunk = y_chunk = lambda c: c

    def io_spec(seq_major, chunk):
        if seq_major:
            return pl.BlockSpec((nb, tc, D_MODEL), lambda c: (0, chunk(c), 0))
        return pl.BlockSpec((m, D_MODEL), lambda c: (chunk(c), 0))

    in_specs = ([io_spec(x_seq_major, x_chunk)]
                + [_in_hbm() for _ in range(4)] + [_resident() for _ in range(22)])
    out_specs = (io_spec(y_seq_major, y_chunk),) + tuple(_in_hbm() for _ in range(4))
    out_shape = (
        jax.ShapeDtypeStruct((nb, rows // nb, D_MODEL) if y_seq_major else (rows, D_MODEL), _F32),
        jax.ShapeDtypeStruct((nb, S5_NSTATE), _F32), jax.ShapeDtypeStruct((nb, S5_NSTATE), _F32),
        jax.ShapeDtypeStruct((hist, CONV_CH), _F32), jax.ShapeDtypeStruct((ftail, 2 * D_FF), _F32),
    )
    scratch = [
        pltpu.VMEM((nb, S5_NSTATE), _F32),
        pltpu.VMEM((nb, S5_NSTATE), _F32),
        pltpu.VMEM((hist + m, CONV_CH), _F32),
        pltpu.VMEM((ftail, 2 * D_FF), _F32),
        pltpu.VMEM((4 * SUB, D_MODEL), _F32),
        pltpu.VMEM((5 * SUB, CONV_CH), _F32),
        pltpu.VMEM((CONV_K * SUB, CONV_CH), _F32),
        pltpu.VMEM(((FFN_K + 1) * SUB, 2 * D_FF), _F32),
        pltpu.VMEM((2 * SUB, S5_NSTATE), _F32),
        pltpu.VMEM((m, D_MODEL) if x_seq_major else (SUB, LANES), _F32),
        pltpu.VMEM((m, D_MODEL), _BF16),
        pltpu.VMEM((m, D_MODEL), _BF16),
        pltpu.VMEM((m, MIX_IN), _F32),
        pltpu.VMEM((m, CONV_CH), _F32),
        pltpu.VMEM((2, m, 2 * S5_BLOCK_STATE), _F32),
        pltpu.VMEM((2, m, 2 * S5_BLOCK_STATE), _BF16),
        pltpu.VMEM((m, S5_WIDTH), _F32),
        pltpu.VMEM((m, D_MODEL), _BF16),
        pltpu.VMEM((m, D_MODEL), _F32),
        pltpu.VMEM((m, D_MODEL), _F32),
        pltpu.VMEM((FF_SLOTS, ftail + m, 2 * FF_BLOCK), _F32),
        pltpu.VMEM((m, D_FF), _BF16),
        pltpu.VMEM((m, D_MODEL), _F32),
    ]
    kern = functools.partial(_layer_kernel, nb, tc, n_chunks, pipelined, layer, x_seq_major, y_seq_major)
    return pl.pallas_call(
        kern,
        grid=(n_steps,),
        in_specs=in_specs,
        out_specs=out_specs,
        out_shape=out_shape,
        scratch_shapes=scratch,
        compiler_params=pltpu.CompilerParams(
            dimension_semantics=("arbitrary",), vmem_limit_bytes=VMEM_LIMIT_BYTES),
        name=f"trunk_layer_nb{nb}",
    )(x, *states, *w)


def _block_diag_in(bb):
    t = bb.reshape(S5_GROUP, S5_BLOCKS, 8, S5_STATE).transpose(1, 2, 0, 3)
    eye = jnp.eye(8, dtype=bb.dtype)
    full = t[:, :, :, None, :] * eye[None, :, None, :, None]
    return full.reshape(S5_BLOCKS, 8 * S5_GROUP, 8 * S5_STATE)


def _block_diag_out(c):
    t = c.reshape(S5_BLOCKS, 8, S5_GROUP, S5_STATE).transpose(0, 1, 3, 2)
    eye = jnp.eye(8, dtype=c.dtype)
    full = t[:, :, :, None, :] * eye[None, :, None, :, None]
    return full.reshape(S5_BLOCKS, 8 * S5_STATE, 8 * S5_GROUP)


def kernel(x_prompt, x_sample, state_ssm_re, state_ssm_im, state_conv, state_ffn_conv, g_pre_mix, w_in, lam_re, lam_im, log_dt, b_re, b_im, c_re, c_im, d_skip, w_glu, b_glu, conv_w, conv_b, ln_g, ln_b, w_out, g_post_mix, g_pre_ffn, w_up, ffn_conv_w, ffn_conv_b, w_down, g_post_ffn):
    batch, seq, _ = x_prompt.shape

    ar, ai, bbr, bbi = _s5_prep(lam_re, lam_im, log_dt, b_re, b_im)

    def layer_weights(l):
        bbar = jnp.concatenate([_block_diag_in(bbr[l]), _block_diag_in(bbi[l])], axis=-1).astype(_BF16)
        return (
            g_pre_mix, w_in[l].astype(_BF16), ar, ai, bbar,
            _block_diag_out(c_re[l]).astype(_BF16), _block_diag_out(c_im[l]).astype(_BF16), d_skip,
            w_glu[l].astype(_BF16), b_glu, conv_w, conv_b, ln_g, ln_b,
            w_out[l].astype(_BF16), g_post_mix, g_pre_ffn, w_up[l].astype(_BF16),
            ffn_conv_w, ffn_conv_b, w_down[l].astype(_BF16), g_post_ffn,
        )

    weights = [layer_weights(l) for l in range(DEPTH)]

    def run_trunk(x, ssm_re, ssm_im, conv_buf, ffn_buf, tc):
        nb, length, _ = x.shape
        in_kernel = nb == SUB
        act = x if in_kernel else x.transpose(1, 0, 2).reshape(length * nb, D_MODEL)
        states = (
            ssm_re.reshape(DEPTH, nb, S5_NSTATE), ssm_im.reshape(DEPTH, nb, S5_NSTATE),
            conv_buf.transpose(0, 2, 1, 3).reshape(DEPTH, (CONV_K - 1) * nb, CONV_CH),
            ffn_buf.transpose(0, 2, 1, 3).reshape(DEPTH, (FFN_K - 1) * nb, 2 * D_FF))
        new = []
        for l in range(DEPTH):
            act, *layer_new = _run_layer(nb, tc, l, act, states, weights[l],
                                         x_seq_major=in_kernel and l == 0,
                                         y_seq_major=in_kernel and l == DEPTH - 1)
            new.append(layer_new)
        hre, him, conv_n, ffn_n = (jnp.stack([n[i] for n in new]) for i in range(4))
        y = act if in_kernel else act.reshape(length, nb, D_MODEL).transpose(1, 0, 2)
        return (y, hre.reshape(DEPTH, nb, S5_GROUPS, S5_STATE), him.reshape(DEPTH, nb, S5_GROUPS, S5_STATE),
                conv_n.reshape(DEPTH, CONV_K - 1, nb, CONV_CH).transpose(0, 2, 1, 3),
                ffn_n.reshape(DEPTH, FFN_K - 1, nb, 2 * D_FF).transpose(0, 2, 1, 3))

    dt = x_prompt.dtype
    zeros_ssm = jnp.zeros((DEPTH, batch, S5_GROUPS, S5_STATE), _F32)
    zeros_conv = jnp.zeros((DEPTH, batch, CONV_K - 1, CONV_CH), dt)
    zeros_ffn = jnp.zeros((DEPTH, batch, FFN_K - 1, 2 * D_FF), dt)
    y_p, p_re, p_im, p_conv, p_ffn = run_trunk(x_prompt, zeros_ssm, zeros_ssm, zeros_conv, zeros_ffn, tc=64)
    y_s, s_re, s_im, s_conv, s_ffn = run_trunk(x_sample, state_ssm_re, state_ssm_im, state_conv, state_ffn_conv, tc=1)
    return (y_p, y_s, p_re, p_im, p_conv, p_ffn, s_re, s_im, s_conv, s_ffn)
```

```python
import functools
import math

import jax
import jax.numpy as jnp
from jax import lax
from jax.experimental import pallas as pl
from jax.experimental.pallas import tpu as pltpu

D_MODEL = 1024
DEPTH = 2
S5_WIDTH = 512
S5_GROUP = 16
S5_GROUPS = 32
S5_STATE = 64
S5_NSTATE = S5_GROUPS * S5_STATE
CONV_CH = 512
CONV_K = 31
MIX_IN = S5_WIDTH + 2 * CONV_CH
D_FF = 2816
FFN_K = 3
EPS = 1e-6

SUB = 8
LANES = 128
PACK = 16
S5_BLOCKS = 4
S5_BLOCK_IN = S5_WIDTH // S5_BLOCKS
S5_BLOCK_STATE = S5_NSTATE // S5_BLOCKS
FF_BLOCK = 256
FF_NBLOCKS = D_FF // FF_BLOCK
FF_SLOTS = 6
VMEM_LIMIT_BYTES = 62 * 1024 * 1024

_F32 = jnp.float32
_BF16 = jnp.bfloat16


def _s5_prep_kernel(lr_ref, li_ref, ldt_ref, br_ref, bi_ref, ar_ref, ai_ref, bbr_ref, bbi_ref):
    dt = jnp.exp(ldt_ref[...])
    lr = lr_ref[...]
    li = li_ref[...]
    mag = jnp.exp(lr * dt)
    ar = mag * jnp.cos(li * dt)
    ai = mag * jnp.sin(li * dt)
    den = lr * lr + li * li
    cr = ((ar - 1.0) * lr + ai * li) / den
    ci = (ai * lr - (ar - 1.0) * li) / den
    ar_ref[...] = ar
    ai_ref[...] = ai
    for l in range(DEPTH):
        rows = slice(l * S5_GROUP, (l + 1) * S5_GROUP)
        br = br_ref[rows, :]
        bi = bi_ref[rows, :]
        bbr_ref[rows, :] = cr[l:l + 1, :] * br - ci[l:l + 1, :] * bi
        bbi_ref[rows, :] = cr[l:l + 1, :] * bi + ci[l:l + 1, :] * br


def _s5_prep(lam_re, lam_im, log_dt, b_re, b_im):
    lr = lam_re.reshape(DEPTH, S5_NSTATE)
    li = lam_im.reshape(DEPTH, S5_NSTATE)
    ldt = jnp.broadcast_to(log_dt[:, :, None], (DEPTH, S5_GROUPS, S5_STATE)).reshape(DEPTH, S5_NSTATE)
    h_major = lambda b: b.transpose(0, 3, 1, 2).reshape(DEPTH * S5_GROUP, S5_NSTATE)
    vec = jax.ShapeDtypeStruct((DEPTH, S5_NSTATE), _F32)
    mat = jax.ShapeDtypeStruct((DEPTH * S5_GROUP, S5_NSTATE), _F32)
    ar, ai, bbr, bbi = pl.pallas_call(_s5_prep_kernel, out_shape=(vec, vec, mat, mat), name="s5_prep")(
        lr, li, ldt, h_major(b_re), h_major(b_im))
    shape = (DEPTH, S5_GROUP, S5_GROUPS, S5_STATE)
    return ar, ai, bbr.reshape(shape), bbi.reshape(shape)


class _Task:
    def __init__(self, name, unit, cost, deps, fn):
        self.name, self.unit, self.cost, self.deps, self.fn = name, unit, max(1, cost), deps, fn


def _list_schedule(tasks):
    succ = {t.name: [] for t in tasks}
    for t in tasks:
        for d in t.deps:
            succ[d].append(t.name)
    prio = {}
    for t in reversed(tasks):
        prio[t.name] = t.cost + max([prio[s] for s in succ[t.name]], default=0)
    free = {"M": 0, "V": 0}
    finish, order, pending = {}, [], list(tasks)
    while pending:
        best = None
        for t in pending:
            if all(d in finish for d in t.deps):
                start = max([finish[d] for d in t.deps] + [free[t.unit]])
                key = (start, -prio[t.name])
                if best is None or key < best[0]:
                    best = (key, t)
        (start, _), t = best
        finish[t.name] = free[t.unit] = start + t.cost
        order.append(t)
        pending.remove(t)
    return order


def _sigmoid(x):
    return 1.0 / (1.0 + jnp.exp(-x))


def _wait_for(x, other):
    xb = lax.bitcast_convert_type(x, jnp.int32)
    ob = lax.bitcast_convert_type(other, jnp.int32)
    return lax.bitcast_convert_type(jnp.minimum(xb, jnp.maximum(xb, ob)), x.dtype)


def _transpose8(tiles):
    rows = lax.broadcasted_iota(jnp.int32, tiles[0].shape, 0)
    t = list(tiles)
    for d in (4, 2, 1):
        keep = (rows & d) == 0
        for i in range(SUB):
            if i & d == 0:
                a, b = t[i], t[i | d]
                t[i] = jnp.where(keep, a, pltpu.roll(b, d, 0))
                t[i | d] = jnp.where(keep, pltpu.roll(a, SUB - d, 0), b)
    return t


def _rms_scale(x):
    return x * lax.rsqrt(jnp.mean(x * x, axis=-1, keepdims=True) + EPS)


def _layer_kernel(nb, tc, n_chunks, pipelined, layer, x_seq_major, y_seq_major,
                  x_ref, hre0_ref, him0_ref, conv0_ref, ffn0_ref,
                  g_pre_mix_ref, w_in_ref, ar_ref, ai_ref, bbar_ref, c_re_ref, c_im_ref, d_skip_ref,
                  w_glu_ref, b_glu_ref, conv_w_ref, conv_b_ref, ln_g_ref, ln_b_ref, w_out_ref,
                  g_post_mix_ref, g_pre_ffn_ref, w_up_ref, fcw_ref, fcb_ref, w_down_ref, g_post_ffn_ref,
                  y_ref, hre_out_ref, him_out_ref, conv_out_ref, ffn_out_ref,
                  hre_s, him_s, zext_s, tail_s,
                  par1k_s, par512_s, convw_s, fcw_s, assm_s,
                  xt_s, hba_s, hbb_s, p_s, cpre_s, xs_s, hs_s, y_s, mixin_s, x1_s, x1b_s, e_s, act_s, f_s):
    m = nb * tc
    nbt = nb // SUB
    hist = (CONV_K - 1) * nb
    ftail = (FFN_K - 1) * nb
    multi_chunk = n_chunks > 1
    step = pl.program_id(0)
    last = pl.num_programs(0) - 1
    a_valid = (step < n_chunks) if pipelined else None
    b_valid = (step >= 1) if pipelined else None

    def commit(ref, idx, new, valid):
        ref[idx] = new if valid is None else jnp.where(valid, new, ref[idx])

    @pl.when(step == 0)
    def _prologue():
        pltpu.sync_copy(hre0_ref.at[layer], hre_s)
        pltpu.sync_copy(him0_ref.at[layer], him_s)
        pltpu.sync_copy(conv0_ref.at[layer], zext_s.at[0:hist, :])
        pltpu.sync_copy(ffn0_ref.at[layer], tail_s)

        def rep(src_row):
            return jnp.broadcast_to(src_row, (SUB, src_row.shape[1]))

        def row(ref):
            return ref[layer:layer + 1, :]
        for i, r in enumerate((g_pre_mix_ref, g_post_mix_ref, g_pre_ffn_ref, g_post_ffn_ref)):
            par1k_s[SUB * i:SUB * (i + 1), :] = rep(row(r))
        for i, r in enumerate((conv_b_ref, ln_g_ref, ln_b_ref, d_skip_ref, b_glu_ref)):
            par512_s[SUB * i:SUB * (i + 1), :] = rep(row(r))
        for k in range(CONV_K):
            convw_s[SUB * k:SUB * (k + 1), :] = rep(conv_w_ref[layer, k:k + 1, :])
        for k in range(FFN_K):
            fcw_s[SUB * k:SUB * (k + 1), :] = rep(fcw_ref[layer, k:k + 1, :])
        fcw_s[SUB * FFN_K:SUB * (FFN_K + 1), :] = rep(row(fcb_ref))
        assm_s[0:SUB, :] = rep(row(ar_ref))
        assm_s[SUB:2 * SUB, :] = rep(row(ai_ref))
        if pipelined:
            x1_s[...] = jnp.zeros_like(x1_s)

    def par1k(i):
        return par1k_s[SUB * i:SUB * (i + 1), :]

    def par512(i):
        return par512_s[SUB * i:SUB * (i + 1), :]

    def pair_to_bf16(fn, r):
        return jnp.concatenate([fn(r), fn(r + SUB)], axis=0).astype(_BF16)

    def row_parts(n_parts):
        n_parts = max(1, min(n_parts, m // PACK))
        tiles = m // PACK
        edges = [PACK * ((tiles * i) // n_parts) for i in range(n_parts + 1)]
        return [(edges[i], edges[i + 1]) for i in range(n_parts)]

    tasks = []

    def task(name, unit, cost, deps, fn):
        tasks.append(_Task(name, unit, cost * m // 512, [d for d in deps if d is not None], fn))
        return name

    def staged(prefix, n_parts, cost, deps, fn):
        parts = row_parts(n_parts)
        return [task(f"{prefix}.{i}", "V", cost // len(parts), deps, functools.partial(fn, r0, r1))
                for i, (r0, r1) in enumerate(parts)]

    def ffn_norm(r0, r1):
        g = par1k(2)

        def half(r8):
            x1 = x1_s[r8:r8 + SUB, :]
            x1b_s[r8:r8 + SUB, :] = x1
            return _rms_scale(x1) * g
        for r in range(r0, r1, PACK):
            hbb_s[r:r + PACK, :] = pair_to_bf16(half, r)

    def pre_norm(r0, r1):
        g = par1k(0)
        cache = {}

        def x_slab(r8):
            if not x_seq_major:
                return x_ref[r8:r8 + SUB, :]
            t0 = (r8 // SUB) // SUB * SUB
            if t0 not in cache:
                cache.clear()
                cache[t0] = _transpose8([x_ref[b, t0:t0 + SUB, :] for b in range(SUB)])
            xt_s[r8:r8 + SUB, :] = cache[t0][r8 // SUB - t0]
            return cache[t0][r8 // SUB - t0]
        for r in range(r0, r1, PACK):
            hba_s[r:r + PACK, :] = pair_to_bf16(lambda r8: _rms_scale(x_slab(r8)) * g, r)

    def in_proj():
        p_s[...] = jnp.dot(hba_s[...], w_in_ref[...], preferred_element_type=_F32)

    def conv_glu(r0, r1):
        for r8 in range(r0, r1, SUB):
            v = p_s[r8:r8 + SUB, S5_WIDTH:S5_WIDTH + CONV_CH]
            gt = p_s[r8:r8 + SUB, S5_WIDTH + CONV_CH:MIX_IN]
            zext_s[hist + r8:hist + r8 + SUB, :] = v * _sigmoid(gt)

    conv_chain = []
    up_done = {}

    def conv_ln(r0, r1, q):
        conv_b, ln_g, ln_b = par512(0), par512(1), par512(2)
        if q % 2 == 0 and q // 2 in up_done:
            conv_chain.append(up_done.pop(q // 2))

        def taps_direct(r8):
            acc = zext_s[r8:r8 + SUB, :] * convw_s[0:SUB, :]
            for k in range(1, CONV_K):
                acc = acc + zext_s[r8 + k * nb:r8 + k * nb + SUB, :] * convw_s[SUB * k:SUB * (k + 1), :]
            return acc

        def taps_windowed():
            n = (r1 - r0) // SUB
            for lt in range(CONV_CH // LANES):
                lanes = slice(lt * LANES, (lt + 1) * LANES)
                w = [convw_s[SUB * k:SUB * (k + 1), lanes] for k in range(CONV_K)]
                acc = [None] * n
                for u in range(n + CONV_K - 1):
                    z = zext_s[r0 + SUB * u:r0 + SUB * (u + 1), lanes]
                    if u == 0:
                        while conv_chain:
                            z = _wait_for(z, conv_chain.pop())
                    for i in range(max(0, u - CONV_K + 1), min(n, u + 1)):
                        term = z * w[u - i]
                        acc[i] = term if acc[i] is None else acc[i] + term
                conv_chain.append(acc[-1])
                for i in range(n):
                    cpre_s[r0 + SUB * i:r0 + SUB * (i + 1), lanes] = acc[i]

        windowed = nb == SUB
        if windowed:
            taps_windowed()

        def half(r8):
            cpre = (cpre_s[r8:r8 + SUB, :] if windowed else taps_direct(r8)) + conv_b
            xc = cpre - jnp.mean(cpre, axis=-1, keepdims=True)
            ln = xc * lax.rsqrt(jnp.mean(xc * xc, axis=-1, keepdims=True) + EPS) * ln_g + ln_b
            return ln * _sigmoid(ln)
        for r in range(r0, r1, PACK):
            mixin_s[r:r + PACK, S5_WIDTH:] = pair_to_bf16(half, r)

    def conv_carry():
        commit(zext_s, (slice(0, hist), slice(None)), zext_s[m:m + hist, :], a_valid)

    def s5_in(j):
        ub = p_s[:, j * S5_BLOCK_IN:(j + 1) * S5_BLOCK_IN].astype(_BF16)
        xs_s[j % 2] = jnp.dot(ub, bbar_ref[j], preferred_element_type=_F32)

    def s5_scan(j):
        cols = slice(j * S5_BLOCK_STATE, (j + 1) * S5_BLOCK_STATE)
        arb = assm_s[0:SUB, cols]
        aib = assm_s[SUB:2 * SUB, cols]
        state = [(hre_s[SUB * bt:SUB * (bt + 1), cols], him_s[SUB * bt:SUB * (bt + 1), cols])
                 for bt in range(nbt)]
        for r in range(0, m, PACK):
            new_r, new_i = [], []
            for r8 in (r, r + SUB):
                bt = (r8 // SUB) % nbt
                hr, hi = state[bt]
                xr = xs_s[j % 2, r8:r8 + SUB, 0:S5_BLOCK_STATE]
                xi = xs_s[j % 2, r8:r8 + SUB, S5_BLOCK_STATE:]
                hr_n = arb * hr - aib * hi + xr
                hi_n = arb * hi + aib * hr + xi
                state[bt] = (hr_n, hi_n)
                new_r.append(hr_n)
                new_i.append(hi_n)
            hs_s[j % 2, r:r + PACK, 0:S5_BLOCK_STATE] = jnp.concatenate(new_r, axis=0).astype(_BF16)
            hs_s[j % 2, r:r + PACK, S5_BLOCK_STATE:] = jnp.concatenate(new_i, axis=0).astype(_BF16)
        for bt in range(nbt):
            rows = slice(SUB * bt, SUB * (bt + 1))
            commit(hre_s, (rows, cols), state[bt][0], a_valid)
            commit(him_s, (rows, cols), state[bt][1], a_valid)

    def s5_out(j):
        y_s[:, j * S5_BLOCK_IN:(j + 1) * S5_BLOCK_IN] = (
            jnp.dot(hs_s[j % 2, :, 0:S5_BLOCK_STATE], c_re_ref[j], preferred_element_type=_F32)
            - jnp.dot(hs_s[j % 2, :, S5_BLOCK_STATE:], c_im_ref[j], preferred_element_type=_F32))

    def gelu(r0, r1):
        d_skip = par512(3)
        sqrt_half = math.sqrt(0.5)

        def half(r8):
            yv = y_s[r8:r8 + SUB, :] + d_skip * p_s[r8:r8 + SUB, 0:S5_WIDTH]
            a = 0.5 * yv * (1.0 + lax.erf(yv * sqrt_half))
            y_s[r8:r8 + SUB, :] = a
            return a
        for r in range(r0, r1, PACK):
            mixin_s[r:r + PACK, 0:S5_WIDTH] = pair_to_bf16(half, r)

    def glu_proj():
        p_s[:, 0:S5_WIDTH] = jnp.dot(mixin_s[:, 0:S5_WIDTH], w_glu_ref[...], preferred_element_type=_F32)

    def glu_gate(r0, r1):
        b_glu = par512(4)
        for r in range(r0, r1, PACK):
            mixin_s[r:r + PACK, 0:S5_WIDTH] = pair_to_bf16(
                lambda r8: y_s[r8:r8 + SUB, :] * _sigmoid(p_s[r8:r8 + SUB, 0:S5_WIDTH] + b_glu), r)

    def out_proj():
        p_s[:, 0:D_MODEL] = jnp.dot(mixin_s[...], w_out_ref[...], preferred_element_type=_F32)

    def mix_residual(r0, r1):
        g2 = par1k(1)
        x_rows = xt_s if x_seq_major else x_ref
        for r8 in range(r0, r1, SUB):
            x1_s[r8:r8 + SUB, :] = x_rows[r8:r8 + SUB, :] + _rms_scale(p_s[r8:r8 + SUB, 0:D_MODEL]) * g2

    def ff_cols(j):
        return slice(j * FF_BLOCK, (j + 1) * FF_BLOCK), slice(D_FF + j * FF_BLOCK, D_FF + (j + 1) * FF_BLOCK)

    def ff_up(j):
        slot = j % FF_SLOTS
        for cols, lanes in zip(ff_cols(j), (slice(0, FF_BLOCK), slice(FF_BLOCK, 2 * FF_BLOCK))):
            e_s[slot, ftail:ftail + m, lanes] = jnp.dot(hbb_s[...], w_up_ref[:, cols], preferred_element_type=_F32)
            e_s[slot, 0:ftail, lanes] = tail_s[:, cols]
            commit(tail_s, (slice(None), cols), e_s[slot, m:m + ftail, lanes], b_valid)
        if pipelined and nb == SUB:
            up_done[j] = e_s[slot, ftail + m - SUB:ftail + m, 2 * FF_BLOCK - LANES:]

    def ff_act(j, r0, r1):
        slot = j % FF_SLOTS
        gate_cols, value_cols = ff_cols(j)

        def conv3(lanes, cols):
            taps = [fcw_s[SUB * k:SUB * (k + 1), cols] for k in range(FFN_K + 1)]
            tiles = {}

            def tile(r8):
                if r8 not in tiles:
                    tiles[r8] = e_s[slot, r8:r8 + SUB, lanes]
                return tiles[r8]
            return {r8: (tile(r8) * taps[0] + tile(r8 + nb) * taps[1] + tile(r8 + 2 * nb) * taps[2]) + taps[3]
                    for r8 in range(r0, r1, SUB)}

        cg = conv3(slice(0, FF_BLOCK), gate_cols)
        cv = conv3(slice(FF_BLOCK, 2 * FF_BLOCK), value_cols)
        for r in range(r0, r1, PACK):
            act_s[r:r + PACK, gate_cols] = pair_to_bf16(lambda r8: cg[r8] * _sigmoid(cg[r8]) * cv[r8], r)

    def ff_down():
        f_s[...] = jnp.dot(act_s[...], w_down_ref[...], preferred_element_type=_F32)

    def ffn_residual(r0, r1):
        g4 = par1k(3)

        def slab(r8):
            return x1b_s[r8:r8 + SUB, :] + _rms_scale(f_s[r8:r8 + SUB, :]) * g4
        if not y_seq_major:
            for r8 in range(r0, r1, SUB):
                y_ref[r8:r8 + SUB, :] = slab(r8)
            return
        for r64 in range(r0, r1, SUB * SUB):
            t0 = r64 // SUB
            tiles = _transpose8([slab(r64 + SUB * s) for s in range(SUB)])
            for b in range(SUB):
                y_ref[b, t0:t0 + SUB, :] = tiles[b]

    t_ups = []

    def ffn_tasks(norm_deps):
        t_fnorm = staged("ffn_norm", 2, 600, norm_deps, ffn_norm)
        t_act = []
        for j in range(FF_NBLOCKS):
            t_ups.append(task(f"ff_up.{j}", "M", 1024, t_fnorm + (t_act[j - FF_SLOTS] if j >= FF_SLOTS else []),
                              functools.partial(ff_up, j)))
            t_act.append(staged(f"ff_act.{j}", 2, 800, t_ups[-1:], functools.partial(ff_act, j)))
        t_down = task("ff_down", "M", 5632, [t for ts in t_act for t in ts], ff_down)
        staged("ffn_residual", 2, 600, [t_down], ffn_residual)
        return t_fnorm

    if pipelined:
        t_fnorm = ffn_tasks([])
    t_pre = staged("pre_norm", 2, 600, [], pre_norm)
    t_in = task("in_proj", "M", 3072, t_pre, in_proj)
    t_glu = staged("conv_glu", 1, 400, [t_in], conv_glu)
    t_conv = [task(f"conv_ln.{q}", "V", 325, t_glu + t_ups[q // 2:q // 2 + 1], functools.partial(conv_ln, r0, r1, q))
              for q, (r0, r1) in enumerate(row_parts(16))]
    if multi_chunk:
        t_conv = [task("conv_carry", "V", 60, t_conv, conv_carry)]
    t_scan, t_sout = [], []
    for j in range(S5_BLOCKS):
        t_sin = task(f"s5_in.{j}", "M", 512, [t_in] + t_scan[-2:-1], functools.partial(s5_in, j))
        t_scan.append(task(f"s5_scan.{j}", "V", 900, [t_sin] + t_sout[-2:-1], functools.partial(s5_scan, j)))
        t_sout.append(task(f"s5_out.{j}", "M", 512, [t_scan[-1]], functools.partial(s5_out, j)))
    t_gelu = staged("gelu", 2, 800, t_sout, gelu)
    t_gproj = task("glu_proj", "M", 512, t_gelu, glu_proj)
    t_gate = staged("glu_gate", 1, 400, [t_gproj], glu_gate)
    t_out = task("out_proj", "M", 2048, t_gate + t_conv, out_proj)
    if pipelined:
        staged("mix_residual", 2, 800, [t_out] + t_fnorm, mix_residual)
    else:
        ffn_tasks(staged("mix_residual", 2, 800, [t_out], mix_residual))

    for t in _list_schedule(tasks):
        t.fn()

    @pl.when(step == last)
    def _epilogue():
        pltpu.sync_copy(hre_s, hre_out_ref)
        pltpu.sync_copy(him_s, him_out_ref)
        if multi_chunk:
            pltpu.sync_copy(zext_s.at[0:hist, :], conv_out_ref)
        else:
            pltpu.sync_copy(zext_s.at[m:m + hist, :], conv_out_ref)
        pltpu.sync_copy(tail_s, ffn_out_ref)


def _in_hbm():
    return pl.BlockSpec(memory_space=pl.ANY)


def _resident():
    return pl.BlockSpec(memory_space=pltpu.MemorySpace.VMEM)


def _run_layer(nb, tc, layer, x, states, w, x_seq_major=False, y_seq_major=False):
    rows = x.shape[0] * x.shape[1] if x_seq_major else x.shape[0]
    m = nb * tc
    assert nb == SUB or not (x_seq_major or y_seq_major)
    assert rows % m == 0 and m % PACK == 0 and nb % SUB == 0
    n_chunks = rows // m
    pipelined = n_chunks > 1
    hist = (CONV_K - 1) * nb
    ftail = (FFN_K - 1) * nb
    assert n_chunks == 1 or m >= hist

    if pipelined:
        n_steps = n_chunks + 1
        x_chunk = lambda c: jnp.minimum(c, n_chunks - 1)
        y_chunk = lambda c: jnp.maximum(c - 1, 0)
    else:
        n_steps = n_chunks
        x_chunk = y_chunk = lambda c: c

    def io_spec(seq_major, chunk):
        if seq_major:
            return pl.BlockSpec((nb, tc, D_MODEL), lambda c: (0, chunk(c), 0))
        return pl.BlockSpec((m, D_MODEL), lambda c: (chunk(c), 0))

    in_specs = ([io_spec(x_seq_major, x_chunk)]
                + [_in_hbm() for _ in range(4)] + [_resident() for _ in range(22)])
    out_specs = (io_spec(y_seq_major, y_chunk),) + tuple(_in_hbm() for _ in range(4))
    out_shape = (
        jax.ShapeDtypeStruct((nb, rows // nb, D_MODEL) if y_seq_major else (rows, D_MODEL), _F32),
        jax.ShapeDtypeStruct((nb, S5_NSTATE), _F32), jax.ShapeDtypeStruct((nb, S5_NSTATE), _F32),
        jax.ShapeDtypeStruct((hist, CONV_CH), _F32), jax.ShapeDtypeStruct((ftail, 2 * D_FF), _F32),
    )
    scratch = [
        pltpu.VMEM((nb, S5_NSTATE), _F32),
        pltpu.VMEM((nb, S5_NSTATE), _F32),
        pltpu.VMEM((hist + m, CONV_CH), _F32),
        pltpu.VMEM((ftail, 2 * D_FF), _F32),
        pltpu.VMEM((4 * SUB, D_MODEL), _F32),
        pltpu.VMEM((5 * SUB, CONV_CH), _F32),
        pltpu.VMEM((CONV_K * SUB, CONV_CH), _F32),
        pltpu.VMEM(((FFN_K + 1) * SUB, 2 * D_FF), _F32),
        pltpu.VMEM((2 * SUB, S5_NSTATE), _F32),
        pltpu.VMEM((m, D_MODEL) if x_seq_major else (SUB, LANES), _F32),
        pltpu.VMEM((m, D_MODEL), _BF16),
        pltpu.VMEM((m, D_MODEL), _BF16),
        pltpu.VMEM((m, MIX_IN), _F32),
        pltpu.VMEM((m, CONV_CH), _F32),
        pltpu.VMEM((2, m, 2 * S5_BLOCK_STATE), _F32),
        pltpu.VMEM((2, m, 2 * S5_BLOCK_STATE), _BF16),
        pltpu.VMEM((m, S5_WIDTH), _F32),
        pltpu.VMEM((m, D_MODEL), _BF16),
        pltpu.VMEM((m, D_MODEL), _F32),
        pltpu.VMEM((m, D_MODEL), _F32),
        pltpu.VMEM((FF_SLOTS, ftail + m, 2 * FF_BLOCK), _F32),
        pltpu.VMEM((m, D_FF), _BF16),
        pltpu.VMEM((m, D_MODEL), _F32),
    ]
    kern = functools.partial(_layer_kernel, nb, tc, n_chunks, pipelined, layer, x_seq_major, y_seq_major)
    return pl.pallas_call(
        kern,
        grid=(n_steps,),
        in_specs=in_specs,
        out_specs=out_specs,
        out_shape=out_shape,
        scratch_shapes=scratch,
        compiler_params=pltpu.CompilerParams(
            dimension_semantics=("arbitrary",), vmem_limit_bytes=VMEM_LIMIT_BYTES),
        name=f"trunk_layer_nb{nb}",
    )(x, *states, *w)


def _block_diag_in(bb):
    t = bb.reshape(S5_GROUP, S5_BLOCKS, 8, S5_STATE).transpose(1, 2, 0, 3)
    eye = jnp.eye(8, dtype=bb.dtype)
    full = t[:, :, :, None, :] * eye[None, :, None, :, None]
    return full.reshape(S5_BLOCKS, 8 * S5_GROUP, 8 * S5_STATE)


def _block_diag_out(c):
    t = c.reshape(S5_BLOCKS, 8, S5_GROUP, S5_STATE).transpose(0, 1, 3, 2)
    eye = jnp.eye(8, dtype=c.dtype)
    full = t[:, :, :, None, :] * eye[None, :, None, :, None]
    return full.reshape(S5_BLOCKS, 8 * S5_STATE, 8 * S5_GROUP)


def kernel(x_prompt, x_sample, state_ssm_re, state_ssm_im, state_conv, state_ffn_conv, g_pre_mix, w_in, lam_re, lam_im, log_dt, b_re, b_im, c_re, c_im, d_skip, w_glu, b_glu, conv_w, conv_b, ln_g, ln_b, w_out, g_post_mix, g_pre_ffn, w_up, ffn_conv_w, ffn_conv_b, w_down, g_post_ffn):
    batch, seq, _ = x_prompt.shape

    ar, ai, bbr, bbi = _s5_prep(lam_re, lam_im, log_dt, b_re, b_im)

    def layer_weights(l):
        bbar = jnp.concatenate([_block_diag_in(bbr[l]), _block_diag_in(bbi[l])], axis=-1).astype(_BF16)
        return (
            g_pre_mix, w_in[l].astype(_BF16), ar, ai, bbar,
            _block_diag_out(c_re[l]).astype(_BF16), _block_diag_out(c_im[l]).astype(_BF16), d_skip,
            w_glu[l].astype(_BF16), b_glu, conv_w, conv_b, ln_g, ln_b,
            w_out[l].astype(_BF16), g_post_mix, g_pre_ffn, w_up[l].astype(_BF16),
            ffn_conv_w, ffn_conv_b, w_down[l].astype(_BF16), g_post_ffn,
        )

    weights = [layer_weights(l) for l in range(DEPTH)]

    def run_trunk(x, ssm_re, ssm_im, conv_buf, ffn_buf, tc):
        nb, length, _ = x.shape
        in_kernel = nb == SUB
        act = x if in_kernel else x.transpose(1, 0, 2).reshape(length * nb, D_MODEL)
        states = (
            ssm_re.reshape(DEPTH, nb, S5_NSTATE), ssm_im.reshape(DEPTH, nb, S5_NSTATE),
            conv_buf.transpose(0, 2, 1, 3).reshape(DEPTH, (CONV_K - 1) * nb, CONV_CH),
            ffn_buf.transpose(0, 2, 1, 3).reshape(DEPTH, (FFN_K - 1) * nb, 2 * D_FF))
        new = []
        for l in range(DEPTH):
            act, *layer_new = _run_layer(nb, tc, l, act, states, weights[l],
                                         x_seq_major=in_kernel and l == 0,
                                         y_seq_major=in_kernel and l == DEPTH - 1)
            new.append(layer_new)
        hre, him, conv_n, ffn_n = (jnp.stack([n[i] for n in new]) for i in range(4))
        y = act if in_kernel else act.reshape(length, nb, D_MODEL).transpose(1, 0, 2)
        return (y, hre.reshape(DEPTH, nb, S5_GROUPS, S5_STATE), him.reshape(DEPTH, nb, S5_GROUPS, S5_STATE),
                conv_n.reshape(DEPTH, CONV_K - 1, nb, CONV_CH).transpose(0, 2, 1, 3),
                ffn_n.reshape(DEPTH, FFN_K - 1, nb, 2 * D_FF).transpose(0, 2, 1, 3))

    dt = x_prompt.dtype
    zeros_ssm = jnp.zeros((DEPTH, batch, S5_GROUPS, S5_STATE), _F32)
    zeros_conv = jnp.zeros((DEPTH, batch, CONV_K - 1, CONV_CH), dt)
    zeros_ffn = jnp.zeros((DEPTH, batch, FFN_K - 1, 2 * D_FF), dt)
    y_p, p_re, p_im, p_conv, p_ffn = run_trunk(x_prompt, zeros_ssm, zeros_ssm, zeros_conv, zeros_ffn, tc=64)
    y_s, s_re, s_im, s_conv, s_ffn = run_trunk(x_sample, state_ssm_re, state_ssm_im, state_conv, state_ffn_conv, tc=1)
    return (y_p, y_s, p_re, p_im, p_conv, p_ffn, s_re, s_im, s_conv, s_ffn)
```
